```python
import jax, jax.numpy as jnp
from jax import lax
import numpy as np

D_MODEL = 2048
BATCH = 4
SEQ = 8192
DEPTH = 4
DEC_BATCH = 16
DEC_SEQ = 2048
PAST_LEN = 128

GDN_QK_HEADS = 4
GDN_V_HEADS = 8
GDN_HEAD_DIM = 128
GDN_CONV = 5
GDN_CHUNK = 64
FNET_GROUPS = 4
FNET_GROUP_DIM = 256
ATT_Q_HEADS = 32
ATT_KV_HEADS = 4
ATT_HEAD_DIM = 64
WINDOW = 128
ROPE_DIM = ATT_HEAD_DIM // 4
ROPE_THETA = 500000.0
N_EXPERTS = 16
EXPERT_FF = 1024
EC_FACTOR = 2
NORM_EPS = 1e-6

GDN_QK_W = GDN_QK_HEADS * GDN_HEAD_DIM
GDN_V_W = GDN_V_HEADS * GDN_HEAD_DIM
GDN_CONV_CH = 2 * GDN_QK_W + GDN_V_W
FNET_W = FNET_GROUPS * FNET_GROUP_DIM
AB_IN = GDN_CONV_CH + GDN_V_W + 4 * GDN_V_HEADS + FNET_W
AB_MIX = GDN_V_W + FNET_W
ATT_Q_W = ATT_Q_HEADS * ATT_HEAD_DIM
ATT_KV_W = ATT_KV_HEADS * ATT_HEAD_DIM
C_IN = ATT_Q_W + 2 * ATT_KV_W
N_EVEN = (DEPTH + 1) // 2
N_ODD = DEPTH // 2

kernel_name = 'hybrid_bidir_gdn_fnet_swa_ec_moe'


def _rms(x, g):
    xf = x.astype(jnp.float32)
    y = xf * lax.rsqrt(jnp.mean(xf * xf, axis=-1, keepdims=True) + NORM_EPS)
    return (y * g.astype(jnp.float32)).astype(x.dtype)


def _l2(x):
    xf = x.astype(jnp.float32)
    return xf * lax.rsqrt(jnp.sum(xf * xf, axis=-1, keepdims=True) + NORM_EPS)


def _dwconv(x, w):
    pad = (w.shape[0] - 1) // 2
    return lax.conv_general_dilated(x, w[:, None, :].astype(x.dtype), window_strides=(1,),
                                    padding=[(pad, pad)], dimension_numbers=('NWC', 'WIO', 'NWC'),
                                    feature_group_count=x.shape[-1])


def _gdn_chunked(q, k, v, g, beta):
    B, S, H, DK = q.shape
    DV = v.shape[-1]
    C = GDN_CHUNK
    N = S // C
    ch = lambda t: jnp.moveaxis(t.reshape(B, N, C, H, t.shape[-1]), (1, 3), (0, 2))
    chs = lambda t: jnp.moveaxis(t.reshape(B, N, C, H), (1, 3), (0, 2))
    qc, kc, vc = ch(q), ch(k), ch(v)
    gc = jnp.cumsum(chs(g), axis=-1)
    bc = chs(beta)[..., None]
    kb, vb = kc * bc, vc * bc
    idx = jnp.arange(C)
    tril = idx[:, None] >= idx[None, :]
    strict = idx[:, None] > idx[None, :]
    diff = gc[..., :, None] - gc[..., None, :]
    decay = jnp.where(tril, jnp.exp(jnp.where(tril, diff, 0.0)), 0.0)
    L = jnp.where(strict, jnp.einsum('nbhid,nbhjd->nbhij', kb, kc) * decay, 0.0)
    eye = jnp.eye(C, dtype=L.dtype)
    T = lax.linalg.triangular_solve(L + eye, jnp.broadcast_to(eye, L.shape), left_side=True,
                                    lower=True, unit_diagonal=True)
    u = jnp.einsum('nbhij,nbhje->nbhie', T, vb)
    w = jnp.einsum('nbhij,nbhjd->nbhid', T, kb * jnp.exp(gc)[..., None])
    a_intra = jnp.einsum('nbhid,nbhjd->nbhij', qc, kc) * decay
    g_last = gc[..., -1]
    k_tail = kc * jnp.exp(g_last[..., None] - gc)[..., None]
    q_head = qc * jnp.exp(gc)[..., None]

    def step(state, xs):
        q_n, w_n, u_n, a_n, k_n, gl_n = xs
        v_new = u_n - jnp.einsum('bhcd,bhde->bhce', w_n, state)
        o = jnp.einsum('bhcd,bhde->bhce', q_n, state) + jnp.einsum('bhij,bhje->bhie', a_n, v_new)
        state = state * jnp.exp(gl_n)[..., None, None] + jnp.einsum('bhcd,bhce->bhde', k_n, v_new)
        return state, o

    s0 = jnp.zeros((B, H, DK, DV), jnp.float32)
    _, o = lax.scan(step, s0, (q_head, w, u, a_intra, k_tail, g_last))
    return jnp.moveaxis(o, (0, 2), (1, 3)).reshape(B, S, H, DV)


def _mix_ab(h, w_in, conv_w, a_log, dt_bias, gdn_g, w_out):
    B, S, _ = h.shape
    p = h @ w_in
    qkv = jax.nn.silu(_dwconv(p[..., :GDN_CONV_CH], conv_w))
    o1 = GDN_CONV_CH + GDN_V_W
    z = p[..., GDN_CONV_CH:o1].astype(jnp.float32).reshape(B, S, GDN_V_HEADS, GDN_HEAD_DIM)
    gates = p[..., o1:o1 + 4 * GDN_V_HEADS].astype(jnp.float32).reshape(B, S, 4, GDN_V_HEADS)
    u_f = p[..., o1 + 4 * GDN_V_HEADS:]
    rep = GDN_V_HEADS // GDN_QK_HEADS
    q = _l2(qkv[..., :GDN_QK_W].reshape(B, S, GDN_QK_HEADS, GDN_HEAD_DIM)) * (GDN_HEAD_DIM ** -0.5)
    k = _l2(qkv[..., GDN_QK_W:2 * GDN_QK_W].reshape(B, S, GDN_QK_HEADS, GDN_HEAD_DIM))
    q = jnp.repeat(q, rep, axis=2)
    k = jnp.repeat(k, rep, axis=2)
    v = qkv[..., 2 * GDN_QK_W:].astype(jnp.float32).reshape(B, S, GDN_V_HEADS, GDN_HEAD_DIM)
    a_log = a_log.astype(jnp.float32)
    dt_bias = dt_bias.astype(jnp.float32)
    g_f = -jnp.exp(a_log[0]) * jax.nn.softplus(gates[:, :, 0] + dt_bias[0])
    b_f = jax.nn.sigmoid(gates[:, :, 1])
    g_b = -jnp.exp(a_log[1]) * jax.nn.softplus(gates[:, :, 2] + dt_bias[1])
    b_b = jax.nn.sigmoid(gates[:, :, 3])
    flip = lambda t: jnp.flip(t, axis=1)
    o_fwd = _gdn_chunked(q, k, v, g_f, b_f)
    o_bwd = flip(_gdn_chunked(flip(q), flip(k), flip(v), flip(g_b), flip(b_b)))
    o_a = (_rms(o_fwd + o_bwd, gdn_g) * jax.nn.silu(z)).reshape(B, S, GDN_V_W)
    uf = u_f.astype(jnp.float32).reshape(B, S, FNET_GROUPS, FNET_GROUP_DIM)
    o_b = jnp.fft.fftn(uf, axes=(1, 3), norm='ortho').real.reshape(B, S, FNET_W)
    return jnp.concatenate([o_a, o_b], axis=-1).astype(h.dtype) @ w_out


def _rope(x, cos, sin):
    half = ROPE_DIM // 2
    x1, x2 = x[..., :half], x[..., half:ROPE_DIM]
    c, s = cos[None, :, None, :], sin[None, :, None, :]
    return jnp.concatenate([x1 * c - x2 * s, x2 * c + x1 * s, x[..., ROPE_DIM:]], axis=-1)


def _mix_c(h, w_in, qn_g, kn_g, sinks, w_out):
    B, S, _ = h.shape
    G = ATT_Q_HEADS // ATT_KV_HEADS
    p = (h @ w_in).astype(jnp.float32)
    q = p[..., :ATT_Q_W].reshape(B, S, ATT_Q_HEADS, ATT_HEAD_DIM)
    k = p[..., ATT_Q_W:ATT_Q_W + ATT_KV_W].reshape(B, S, ATT_KV_HEADS, ATT_HEAD_DIM)
    v = p[..., ATT_Q_W + ATT_KV_W:].reshape(B, S, ATT_KV_HEADS, ATT_HEAD_DIM)
    pos = jnp.arange(S, dtype=jnp.float32)
    inv = 1.0 / (ROPE_THETA ** (jnp.arange(0, ROPE_DIM, 2, dtype=jnp.float32) / ROPE_DIM))
    ang = pos[:, None] * inv[None, :]
    cos, sin = jnp.cos(ang), jnp.sin(ang)
    q = _rope(_rms(q, qn_g), cos, sin) * (ATT_HEAD_DIM ** -0.5)
    k = _rope(_rms(k, kn_g), cos, sin)
    NB = S // WINDOW
    qb = jnp.moveaxis(q.reshape(B, NB, WINDOW, ATT_KV_HEADS, G, ATT_HEAD_DIM), 1, 0)
    kp = jnp.pad(k, ((0, 0), (WINDOW, WINDOW), (0, 0), (0, 0)))
    vp = jnp.pad(v, ((0, 0), (WINDOW, WINDOW), (0, 0), (0, 0)))
    sink = sinks.astype(jnp.float32).reshape(1, ATT_KV_HEADS, G, 1, 1)
    qi = jnp.arange(WINDOW)
    kt = jnp.arange(3 * WINDOW)
    band = jnp.abs(kt[None, :] - WINDOW - qi[:, None]) <= WINDOW

    def block(args):
        n, q_n = args
        kn = lax.dynamic_slice_in_dim(kp, n * WINDOW, 3 * WINDOW, axis=1)
        vn = lax.dynamic_slice_in_dim(vp, n * WINDOW, 3 * WINDOW, axis=1)
        kpos = (n - 1) * WINDOW + kt
        valid = band & ((kpos >= 0) & (kpos < S))[None, :]
        s = jnp.einsum('bqkgd,btkd->bkgqt', q_n, kn)
        s = jnp.where(valid, s, -jnp.inf)
        m = jnp.maximum(jnp.max(s, axis=-1, keepdims=True), sink)
        pr = jnp.exp(s - m)
        attn = pr / (jnp.sum(pr, axis=-1, keepdims=True) + jnp.exp(sink - m))
        return jnp.einsum('bkgqt,btkd->bqkgd', attn, vn)

    o = lax.map(block, (jnp.arange(NB), qb))
    o = jnp.moveaxis(o, 0, 1).reshape(B, S, ATT_Q_W)
    return o.astype(h.dtype) @ w_out


def _ec_moe(xf, router_w, w_gate, w_up, w_down):
    T, D = xf.shape
    cap = max(1, EC_FACTOR * T // N_EXPERTS)
    aff = jax.nn.softmax((xf @ router_w).astype(jnp.float32), axis=-1)
    gate, idx = lax.top_k(aff.T, cap)
    xe = xf[idx]
    hdn = jax.nn.silu(jnp.einsum('ecd,edf->ecf', xe, w_gate)) * jnp.einsum('ecd,edf->ecf', xe, w_up)
    ye = jnp.einsum('ecf,efd->ecd', hdn, w_down) * gate[..., None].astype(xf.dtype)
    return jnp.zeros_like(xf).at[idx.reshape(-1)].add(ye.reshape(-1, D))


def _trunk(x, c, ada_w, ada_b, norm1_g, norm2_g, ab_w_in, ab_conv_w, gdn_a_log, gdn_dt_bias,
           gdn_norm_g, ab_w_out, c_w_in, c_q_norm_g, c_k_norm_g, c_sinks, c_w_out, router_w,
           exp_w_gate, exp_w_up, exp_w_down):
    B, S, D = x.shape
    cs = jax.nn.silu(c)
    for l in range(DEPTH):
        mod = (cs @ ada_w[l] + ada_b[l])[:, None, :]
        sh1, sc1, gt1, sh2, sc2, gt2 = jnp.split(mod, 6, axis=-1)
        hm = _rms(x, norm1_g[l]) * (1.0 + sc1) + sh1
        if l % 2 == 0:
            e = l // 2
            y = _mix_ab(hm, ab_w_in[e], ab_conv_w[e], gdn_a_log[e], gdn_dt_bias[e], gdn_norm_g[e], ab_w_out[e])
        else:
            o = l // 2
            y = _mix_c(hm, c_w_in[o], c_q_norm_g[o], c_k_norm_g[o], c_sinks[o], c_w_out[o])
        x = x + gt1 * y
        hf = _rms(x, norm2_g[l]) * (1.0 + sc2) + sh2
        x = x + gt2 * _ec_moe(hf.reshape(B * S, D), router_w[l], exp_w_gate[l], exp_w_up[l], exp_w_down[l]).reshape(B, S, D)
    return x


def setup_inputs(seed: int = 0) -> dict:
    key = jax.random.key(seed)
    ks = jax.random.split(key, 23)
    f32 = jnp.float32
    D = D_MODEL
    nrm = lambda k, shape, s: jax.random.normal(k, shape, f32) * s
    return {
        'x_prompt': nrm(ks[0], (BATCH, SEQ, D), 1.0),
        'x_sample': nrm(ks[1], (DEC_BATCH, DEC_SEQ, D), 1.0),
        'c_prompt': nrm(ks[2], (BATCH, D), 1.0),
        'c_sample': nrm(ks[3], (DEC_BATCH, D), 1.0),
        'ada_w': nrm(ks[4], (DEPTH, D, 6 * D), 0.5 * D ** -0.5),
        'ada_b': nrm(ks[5], (DEPTH, 6 * D), 0.02),
        'norm1_g': 1.0 + nrm(ks[6], (DEPTH, D), 0.02),
        'norm2_g': 1.0 + nrm(ks[7], (DEPTH, D), 0.02),
        'ab_w_in': nrm(ks[8], (N_EVEN, D, AB_IN), D ** -0.5),
        'ab_conv_w': nrm(ks[9], (N_EVEN, GDN_CONV, GDN_CONV_CH), GDN_CONV ** -0.5),
        'gdn_a_log': jnp.log(jax.random.uniform(ks[10], (N_EVEN, 2, GDN_V_HEADS), f32, 1.0, 16.0)),
        'gdn_dt_bias': nrm(ks[11], (N_EVEN, 2, GDN_V_HEADS), 0.1),
        'gdn_norm_g': 1.0 + nrm(ks[12], (N_EVEN, GDN_HEAD_DIM), 0.02),
        'ab_w_out': nrm(ks[13], (N_EVEN, AB_MIX, D), AB_MIX ** -0.5),
        'c_w_in': nrm(ks[14], (N_ODD, D, C_IN), D ** -0.5),
        'c_q_norm_g': 1.0 + nrm(ks[15], (N_ODD, ATT_HEAD_DIM), 0.02),
        'c_k_norm_g': 1.0 + nrm(ks[16], (N_ODD, ATT_HEAD_DIM), 0.02),
        'c_sinks': nrm(ks[17], (N_ODD, ATT_Q_HEADS), 0.5),
        'c_w_out': nrm(ks[18], (N_ODD, ATT_Q_W, D), ATT_Q_W ** -0.5),
        'router_w': nrm(ks[19], (DEPTH, D, N_EXPERTS), D ** -0.5),
        'exp_w_gate': nrm(ks[20], (DEPTH, N_EXPERTS, D, EXPERT_FF), D ** -0.5),
        'exp_w_up': nrm(ks[21], (DEPTH, N_EXPERTS, D, EXPERT_FF), D ** -0.5),
        'exp_w_down': nrm(ks[22], (DEPTH, N_EXPERTS, EXPERT_FF, D), EXPERT_FF ** -0.5),
    }


def reference(x_prompt, x_sample, c_prompt, c_sample, ada_w, ada_b, norm1_g, norm2_g, ab_w_in,
              ab_conv_w, gdn_a_log, gdn_dt_bias, gdn_norm_g, ab_w_out, c_w_in, c_q_norm_g,
              c_k_norm_g, c_sinks, c_w_out, router_w, exp_w_gate, exp_w_up, exp_w_down):
    y_prompt = _trunk(x_prompt, c_prompt, ada_w, ada_b, norm1_g, norm2_g, ab_w_in, ab_conv_w,
                      gdn_a_log, gdn_dt_bias, gdn_norm_g, ab_w_out, c_w_in, c_q_norm_g, c_k_norm_g,
                      c_sinks, c_w_out, router_w, exp_w_gate, exp_w_up, exp_w_down)
    y_sample = _trunk(x_sample, c_sample, ada_w, ada_b, norm1_g, norm2_g, ab_w_in, ab_conv_w,
                      gdn_a_log, gdn_dt_bias, gdn_norm_g, ab_w_out, c_w_in, c_q_norm_g, c_k_norm_g,
                      c_sinks, c_w_out, router_w, exp_w_gate, exp_w_up, exp_w_down)
    return (y_prompt, y_sample)
```

```python
import functools

import jax
import jax.numpy as jnp
from jax import lax
from jax.experimental import pallas as pl
from jax.experimental.pallas import tpu as pltpu

D_MODEL = 2048
DEPTH = 4
GDN_QK_HEADS = 4
GDN_V_HEADS = 8
GDN_HEAD_DIM = 128
GDN_CHUNK = 64
FNET_GROUPS = 4
FNET_GROUP_DIM = 256
ATT_Q_HEADS = 32
ATT_KV_HEADS = 4
ATT_HEAD_DIM = 64
WINDOW = 128
ROPE_DIM = ATT_HEAD_DIM // 4
ROPE_THETA = 500000.0
N_EXPERTS = 16
EXPERT_FF = 1024
EC_FACTOR = 2
NORM_EPS = 1e-6

GDN_QK_W = GDN_QK_HEADS * GDN_HEAD_DIM
GDN_V_W = GDN_V_HEADS * GDN_HEAD_DIM
GDN_CONV_CH = 2 * GDN_QK_W + GDN_V_W
FNET_W = FNET_GROUPS * FNET_GROUP_DIM
AB_IN = GDN_CONV_CH + GDN_V_W + 4 * GDN_V_HEADS + FNET_W
ATT_Q_W = ATT_Q_HEADS * ATT_HEAD_DIM
ATT_KV_W = ATT_KV_HEADS * ATT_HEAD_DIM

VMEM_LIMIT_BYTES = 48 * 1024 * 1024


def _mm_kernel(a_ref, b_ref, o_ref):
    o_ref[...] = jnp.dot(a_ref[...], b_ref[...], preferred_element_type=jnp.float32)


def _mm(a, b, tm=512, tn=512):
    M, K = a.shape
    N = b.shape[1]
    a = a.astype(jnp.bfloat16)
    b = b.astype(jnp.bfloat16)
    n_pad = (-N) % tn
    if n_pad:
        b = jnp.pad(b, ((0, 0), (0, n_pad)))
    tm = min(tm, M)
    assert M % tm == 0
    out = pl.pallas_call(
        _mm_kernel,
        grid=(M // tm, (N + n_pad) // tn),
        in_specs=[pl.BlockSpec((tm, K), lambda i, j: (i, 0)),
                  pl.BlockSpec((K, tn), lambda i, j: (0, j))],
        out_specs=pl.BlockSpec((tm, tn), lambda i, j: (i, j)),
        out_shape=jax.ShapeDtypeStruct((M, N + n_pad), jnp.float32),
        compiler_params=pltpu.CompilerParams(
            dimension_semantics=("parallel", "parallel"), vmem_limit_bytes=VMEM_LIMIT_BYTES),
    )(a, b)
    return out[:, :N] if n_pad else out


def _bmm_kernel(a_ref, b_ref, o_ref):
    o_ref[...] = jnp.dot(a_ref[...], b_ref[...], preferred_element_type=jnp.float32)


def _bmm(a, b, tm=512, tn=512):
    E, M, K = a.shape
    N = b.shape[2]
    a = a.astype(jnp.bfloat16)
    b = b.astype(jnp.bfloat16)
    return pl.pallas_call(
        _bmm_kernel,
        grid=(E, M // tm, N // tn),
        in_specs=[pl.BlockSpec((None, tm, K), lambda e, i, j: (e, i, 0)),
                  pl.BlockSpec((None, K, tn), lambda e, i, j: (e, 0, j))],
        out_specs=pl.BlockSpec((None, tm, tn), lambda e, i, j: (e, i, j)),
        out_shape=jax.ShapeDtypeStruct((E, M, N), jnp.float32),
        compiler_params=pltpu.CompilerParams(
            dimension_semantics=("parallel", "parallel", "parallel"), vmem_limit_bytes=VMEM_LIMIT_BYTES),
    )(a, b)


def _mm3(h, w):
    B, S, K = h.shape
    return _mm(h.reshape(B * S, K), w).reshape(B, S, w.shape[1])


def _rms(x, g):
    xf = x.astype(jnp.float32)
    y = xf * lax.rsqrt(jnp.mean(xf * xf, axis=-1, keepdims=True) + NORM_EPS)
    return (y * g.astype(jnp.float32)).astype(x.dtype)


def _l2(x):
    xf = x.astype(jnp.float32)
    return xf * lax.rsqrt(jnp.sum(xf * xf, axis=-1, keepdims=True) + NORM_EPS)


def _dwconv(x, w):
    pad = (w.shape[0] - 1) // 2
    return lax.conv_general_dilated(x, w[:, None, :].astype(x.dtype), window_strides=(1,),
                                    padding=[(pad, pad)], dimension_numbers=('NWC', 'WIO', 'NWC'),
                                    feature_group_count=x.shape[-1])


def _gdn_chunked(q, k, v, g, beta):
    B, S, H, DK = q.shape
    DV = v.shape[-1]
    C = GDN_CHUNK
    N = S // C
    ch = lambda t: jnp.moveaxis(t.reshape(B, N, C, H, t.shape[-1]), (1, 3), (0, 2))
    chs = lambda t: jnp.moveaxis(t.reshape(B, N, C, H), (1, 3), (0, 2))
    qc, kc, vc = ch(q), ch(k), ch(v)
    gc = jnp.cumsum(chs(g), axis=-1)
    bc = chs(beta)[..., None]
    kb, vb = kc * bc, vc * bc
    idx = jnp.arange(C)
    tril = idx[:, None] >= idx[None, :]
    strict = idx[:, None] > idx[None, :]
    diff = gc[..., :, None] - gc[..., None, :]
    decay = jnp.where(tril, jnp.exp(jnp.where(tril, diff, 0.0)), 0.0)
    L = jnp.where(strict, jnp.einsum('nbhid,nbhjd->nbhij', kb, kc) * decay, 0.0)
    eye = jnp.eye(C, dtype=L.dtype)
    T = lax.linalg.triangular_solve(L + eye, jnp.broadcast_to(eye, L.shape), left_side=True,
                                    lower=True, unit_diagonal=True)
    u = jnp.einsum('nbhij,nbhje->nbhie', T, vb)
    w = jnp.einsum('nbhij,nbhjd->nbhid', T, kb * jnp.exp(gc)[..., None])
    a_intra = jnp.einsum('nbhid,nbhjd->nbhij', qc, kc) * decay
    g_last = gc[..., -1]
    k_tail = kc * jnp.exp(g_last[..., None] - gc)[..., None]
    q_head = qc * jnp.exp(gc)[..., None]

    def step(state, xs):
        q_n, w_n, u_n, a_n, k_n, gl_n = xs
        v_new = u_n - jnp.einsum('bhcd,bhde->bhce', w_n, state)
        o = jnp.einsum('bhcd,bhde->bhce', q_n, state) + jnp.einsum('bhij,bhje->bhie', a_n, v_new)
        state = state * jnp.exp(gl_n)[..., None, None] + jnp.einsum('bhcd,bhce->bhde', k_n, v_new)
        return state, o

    s0 = jnp.zeros((B, H, DK, DV), jnp.float32)
    _, o = lax.scan(step, s0, (q_head, w, u, a_intra, k_tail, g_last))
    return jnp.moveaxis(o, (0, 2), (1, 3)).reshape(B, S, H, DV)


def _mix_ab(h, w_in, conv_w, a_log, dt_bias, gdn_g, w_out):
    B, S, _ = h.shape
    p = _mm3(h, w_in)
    qkv = jax.nn.silu(_dwconv(p[..., :GDN_CONV_CH], conv_w))
    o1 = GDN_CONV_CH + GDN_V_W
    z = p[..., GDN_CONV_CH:o1].astype(jnp.float32).reshape(B, S, GDN_V_HEADS, GDN_HEAD_DIM)
    gates = p[..., o1:o1 + 4 * GDN_V_HEADS].astype(jnp.float32).reshape(B, S, 4, GDN_V_HEADS)
    u_f = p[..., o1 + 4 * GDN_V_HEADS:]
    rep = GDN_V_HEADS // GDN_QK_HEADS
    q = _l2(qkv[..., :GDN_QK_W].reshape(B, S, GDN_QK_HEADS, GDN_HEAD_DIM)) * (GDN_HEAD_DIM ** -0.5)
    k = _l2(qkv[..., GDN_QK_W:2 * GDN_QK_W].reshape(B, S, GDN_QK_HEADS, GDN_HEAD_DIM))
    q = jnp.repeat(q, rep, axis=2)
    k = jnp.repeat(k, rep, axis=2)
    v = qkv[..., 2 * GDN_QK_W:].astype(jnp.float32).reshape(B, S, GDN_V_HEADS, GDN_HEAD_DIM)
    a_log = a_log.astype(jnp.float32)
    dt_bias = dt_bias.astype(jnp.float32)
    g_f = -jnp.exp(a_log[0]) * jax.nn.softplus(gates[:, :, 0] + dt_bias[0])
    b_f = jax.nn.sigmoid(gates[:, :, 1])
    g_b = -jnp.exp(a_log[1]) * jax.nn.softplus(gates[:, :, 2] + dt_bias[1])
    b_b = jax.nn.sigmoid(gates[:, :, 3])
    flip = lambda t: jnp.flip(t, axis=1)
    o_fwd = _gdn_chunked(q, k, v, g_f, b_f)
    o_bwd = flip(_gdn_chunked(flip(q), flip(k), flip(v), flip(g_b), flip(b_b)))
    o_a = (_rms(o_fwd + o_bwd, gdn_g) * jax.nn.silu(z)).reshape(B, S, GDN_V_W)
    uf = u_f.astype(jnp.float32).reshape(B, S, FNET_GROUPS, FNET_GROUP_DIM)
    o_b = jnp.fft.fftn(uf, axes=(1, 3), norm='ortho').real.reshape(B, S, FNET_W)
    return _mm3(jnp.concatenate([o_a, o_b], axis=-1).astype(h.dtype), w_out)


def _rope(x, cos, sin):
    half = ROPE_DIM // 2
    x1, x2 = x[..., :half], x[..., half:ROPE_DIM]
    c, s = cos[None, :, None, :], sin[None, :, None, :]
    return jnp.concatenate([x1 * c - x2 * s, x2 * c + x1 * s, x[..., ROPE_DIM:]], axis=-1)


def _mix_c(h, w_in, qn_g, kn_g, sinks, w_out):
    B, S, _ = h.shape
    G = ATT_Q_HEADS // ATT_KV_HEADS
    p = _mm3(h, w_in)
    q = p[..., :ATT_Q_W].reshape(B, S, ATT_Q_HEADS, ATT_HEAD_DIM)
    k = p[..., ATT_Q_W:ATT_Q_W + ATT_KV_W].reshape(B, S, ATT_KV_HEADS, ATT_HEAD_DIM)
    v = p[..., ATT_Q_W + ATT_KV_W:].reshape(B, S, ATT_KV_HEADS, ATT_HEAD_DIM)
    pos = jnp.arange(S, dtype=jnp.float32)
    inv = 1.0 / (ROPE_THETA ** (jnp.arange(0, ROPE_DIM, 2, dtype=jnp.float32) / ROPE_DIM))
    ang = pos[:, None] * inv[None, :]
    cos, sin = jnp.cos(ang), jnp.sin(ang)
    q = _rope(_rms(q, qn_g), cos, sin) * (ATT_HEAD_DIM ** -0.5)
    k = _rope(_rms(k, kn_g), cos, sin)
    NB = S // WINDOW
    qb = jnp.moveaxis(q.reshape(B, NB, WINDOW, ATT_KV_HEADS, G, ATT_HEAD_DIM), 1, 0)
    kp = jnp.pad(k, ((0, 0), (WINDOW, WINDOW), (0, 0), (0, 0)))
    vp = jnp.pad(v, ((0, 0), (WINDOW, WINDOW), (0, 0), (0, 0)))
    sink = sinks.astype(jnp.float32).reshape(1, ATT_KV_HEADS, G, 1, 1)
    qi = jnp.arange(WINDOW)
    kt = jnp.arange(3 * WINDOW)
    band = jnp.abs(kt[None, :] - WINDOW - qi[:, None]) <= WINDOW

    def block(args):
        n, q_n = args
        kn = lax.dynamic_slice_in_dim(kp, n * WINDOW, 3 * WINDOW, axis=1)
        vn = lax.dynamic_slice_in_dim(vp, n * WINDOW, 3 * WINDOW, axis=1)
        kpos = (n - 1) * WINDOW + kt
        valid = band & ((kpos >= 0) & (kpos < S))[None, :]
        s = jnp.einsum('bqkgd,btkd->bkgqt', q_n, kn)
        s = jnp.where(valid, s, -jnp.inf)
        m = jnp.maximum(jnp.max(s, axis=-1, keepdims=True), sink)
        pr = jnp.exp(s - m)
        attn = pr / (jnp.sum(pr, axis=-1, keepdims=True) + jnp.exp(sink - m))
        return jnp.einsum('bkgqt,btkd->bqkgd', attn, vn)

    o = lax.map(block, (jnp.arange(NB), qb))
    o = jnp.moveaxis(o, 0, 1).reshape(B, S, ATT_Q_W)
    return _mm3(o.astype(h.dtype), w_out)


def _ec_moe(xf, router_w, w_gate, w_up, w_down):
    T, D = xf.shape
    cap = max(1, EC_FACTOR * T // N_EXPERTS)
    aff = jax.nn.softmax((xf @ router_w).astype(jnp.float32), axis=-1)
    gate, idx = lax.top_k(aff.T, cap)
    xe = xf[idx]
    hdn = jax.nn.silu(_bmm(xe, w_gate)) * _bmm(xe, w_up)
    ye = _bmm(hdn, w_down) * gate[..., None].astype(xf.dtype)
    return jnp.zeros_like(xf).at[idx.reshape(-1)].add(ye.reshape(-1, D))


def _trunk(x, c, ada_w, ada_b, norm1_g, norm2_g, ab_w_in, ab_conv_w, gdn_a_log, gdn_dt_bias,
           gdn_norm_g, ab_w_out, c_w_in, c_q_norm_g, c_k_norm_g, c_sinks, c_w_out, router_w,
           exp_w_gate, exp_w_up, exp_w_down):
    B, S, D = x.shape
    cs = jax.nn.silu(c)
    for l in range(DEPTH):
        mod = (cs @ ada_w[l] + ada_b[l])[:, None, :]
        sh1, sc1, gt1, sh2, sc2, gt2 = jnp.split(mod, 6, axis=-1)
        hm = _rms(x, norm1_g[l]) * (1.0 + sc1) + sh1
        if l % 2 == 0:
            e = l // 2
            y = _mix_ab(hm, ab_w_in[e], ab_conv_w[e], gdn_a_log[e], gdn_dt_bias[e], gdn_norm_g[e], ab_w_out[e])
        else:
            o = l // 2
            y = _mix_c(hm, c_w_in[o], c_q_norm_g[o], c_k_norm_g[o], c_sinks[o], c_w_out[o])
        x = x + gt1 * y
        hf = _rms(x, norm2_g[l]) * (1.0 + sc2) + sh2
        x = x + gt2 * _ec_moe(hf.reshape(B * S, D), router_w[l], exp_w_gate[l], exp_w_up[l], exp_w_down[l]).reshape(B, S, D)
    return x


def kernel(x_prompt, x_sample, c_prompt, c_sample, ada_w, ada_b, norm1_g, norm2_g, ab_w_in, ab_conv_w, gdn_a_log, gdn_dt_bias, gdn_norm_g, ab_w_out, c_w_in, c_q_norm_g, c_k_norm_g, c_sinks, c_w_out, router_w, exp_w_gate, exp_w_up, exp_w_down):
    args = (ada_w, ada_b, norm1_g, norm2_g, ab_w_in, ab_conv_w, gdn_a_log, gdn_dt_bias, gdn_norm_g,
            ab_w_out, c_w_in, c_q_norm_g, c_k_norm_g, c_sinks, c_w_out, router_w, exp_w_gate,
            exp_w_up, exp_w_down)
    return (_trunk(x_prompt, c_prompt, *args), _trunk(x_sample, c_sample, *args))
```

```python
import functools

import jax
import jax.numpy as jnp
from jax import lax
from jax.experimental import pallas as pl
from jax.experimental.pallas import tpu as pltpu

D_MODEL = 2048
DEPTH = 4
GDN_QK_HEADS = 4
GDN_V_HEADS = 8
GDN_HEAD_DIM = 128
GDN_CONV = 5
GDN_CHUNK = 64
FNET_GROUPS = 4
FNET_GROUP_DIM = 256
ATT_Q_HEADS = 32
ATT_KV_HEADS = 4
ATT_HEAD_DIM = 64
WINDOW = 128
ROPE_DIM = ATT_HEAD_DIM // 4
ROPE_THETA = 500000.0
N_EXPERTS = 16
EXPERT_FF = 1024
EC_FACTOR = 2
NORM_EPS = 1e-6

GDN_QK_W = GDN_QK_HEADS * GDN_HEAD_DIM
GDN_V_W = GDN_V_HEADS * GDN_HEAD_DIM
GDN_CONV_CH = 2 * GDN_QK_W + GDN_V_W
FNET_W = FNET_GROUPS * FNET_GROUP_DIM
N_GATES = 4 * GDN_V_HEADS
AB_IN = GDN_CONV_CH + GDN_V_W + N_GATES + FNET_W
ATT_Q_W = ATT_Q_HEADS * ATT_HEAD_DIM
ATT_KV_W = ATT_KV_HEADS * ATT_HEAD_DIM

LANE = 128
SUBLANE = 8
VMEM_LIMIT_BYTES = 48 * 1024 * 1024
NEG_BIG = -1e30


def _dot(a, b):
    return jnp.dot(a, b, preferred_element_type=jnp.float32)


def _mm_kernel(a_ref, b_ref, o_ref):
    o_ref[...] = _dot(a_ref[...], b_ref[...])


def _mm(a, b, tm=512, tn=512):
    M, K = a.shape
    N = b.shape[1]
    assert M % tm == 0 and N % tn == 0
    return pl.pallas_call(
        _mm_kernel,
        grid=(M // tm, N // tn),
        in_specs=[pl.BlockSpec((tm, K), lambda i, j: (i, 0)),
                  pl.BlockSpec((K, tn), lambda i, j: (0, j))],
        out_specs=pl.BlockSpec((tm, tn), lambda i, j: (i, j)),
        out_shape=jax.ShapeDtypeStruct((M, N), jnp.float32),
        compiler_params=pltpu.CompilerParams(
            dimension_semantics=("parallel", "parallel"), vmem_limit_bytes=VMEM_LIMIT_BYTES),
    )(a.astype(jnp.bfloat16), b.astype(jnp.bfloat16))


def _bmm(a, b, tm=512, tn=512):
    E, M, K = a.shape
    N = b.shape[2]
    return pl.pallas_call(
        _mm_kernel,
        grid=(E, M // tm, N // tn),
        in_specs=[pl.BlockSpec((None, tm, K), lambda e, i, j: (e, i, 0)),
                  pl.BlockSpec((None, K, tn), lambda e, i, j: (e, 0, j))],
        out_specs=pl.BlockSpec((None, tm, tn), lambda e, i, j: (e, i, j)),
        out_shape=jax.ShapeDtypeStruct((E, M, N), jnp.float32),
        compiler_params=pltpu.CompilerParams(
            dimension_semantics=("parallel", "parallel", "parallel"), vmem_limit_bytes=VMEM_LIMIT_BYTES),
    )(a.astype(jnp.bfloat16), b.astype(jnp.bfloat16))


def _mm3(h, w, **kw):
    B, S, K = h.shape
    return _mm(h.reshape(B * S, K), w, **kw).reshape(B, S, w.shape[1])


def _rms(x, g):
    xf = x.astype(jnp.float32)
    y = xf * lax.rsqrt(jnp.mean(xf * xf, axis=-1, keepdims=True) + NORM_EPS)
    return (y * g.astype(jnp.float32)).astype(x.dtype)


def _gdn_prep_kernel(prev_ref, cur_ref, next_ref, w_ref, q_ref, k_ref, kt_ref, v_ref, *, ts, n_tiles):
    i = pl.program_id(1)
    has_prev = (i > 0).astype(jnp.float32)
    has_next = (i < n_tiles - 1).astype(jnp.float32)
    half = (GDN_CONV - 1) // 2
    for j in range(GDN_CONV_CH // LANE):
        cols = slice(j * LANE, (j + 1) * LANE)
        ext = jnp.concatenate([prev_ref[:, cols] * has_prev, cur_ref[:, cols], next_ref[:, cols] * has_next], axis=0)
        acc = None
        for t in range(GDN_CONV):
            shift = (half - t) % (ts + 2 * SUBLANE)
            rolled = ext if shift == 0 else pltpu.roll(ext, shift, axis=0)
            term = rolled[SUBLANE:SUBLANE + ts, :] * w_ref[t:t + 1, cols]
            acc = term if acc is None else acc + term
        y = acc * (1.0 / (1.0 + jnp.exp(-acc)))
        if j < 2 * GDN_QK_HEADS:
            n = lax.rsqrt(jnp.sum(y * y, axis=-1, keepdims=True) + NORM_EPS)
            if j < GDN_QK_HEADS:
                q_ref[:, cols] = (y * n * (GDN_HEAD_DIM ** -0.5)).astype(q_ref.dtype)
            else:
                kc = slice((j - GDN_QK_HEADS) * LANE, (j - GDN_QK_HEADS + 1) * LANE)
                kn = y * n
                k_ref[:, kc] = kn.astype(k_ref.dtype)
                kt_ref[kc, :] = kn.T.astype(kt_ref.dtype)
        else:
            vc = slice((j - 2 * GDN_QK_HEADS) * LANE, (j - 2 * GDN_QK_HEADS + 1) * LANE)
            v_ref[:, vc] = y.astype(v_ref.dtype)


def _gdn_prep(p, conv_w, ts=256):
    B, S, _ = p.shape
    n_tiles = S // ts
    r = ts // SUBLANE
    w8 = jnp.pad(conv_w.astype(jnp.float32), ((0, SUBLANE - GDN_CONV), (0, 0)))
    return pl.pallas_call(
        functools.partial(_gdn_prep_kernel, ts=ts, n_tiles=n_tiles),
        grid=(B, n_tiles),
        in_specs=[
            pl.BlockSpec((None, SUBLANE, GDN_CONV_CH), lambda b, i: (b, jnp.maximum(i * r - 1, 0), 0)),
            pl.BlockSpec((None, ts, GDN_CONV_CH), lambda b, i: (b, i, 0)),
            pl.BlockSpec((None, SUBLANE, GDN_CONV_CH), lambda b, i: (b, jnp.minimum((i + 1) * r, S // SUBLANE - 1), 0)),
            pl.BlockSpec((SUBLANE, GDN_CONV_CH), lambda b, i: (0, 0)),
        ],
        out_specs=(
            pl.BlockSpec((None, ts, GDN_QK_W), lambda b, i: (b, i, 0)),
            pl.BlockSpec((None, ts, GDN_QK_W), lambda b, i: (b, i, 0)),
            pl.BlockSpec((None, GDN_QK_W, ts), lambda b, i: (b, 0, i)),
            pl.BlockSpec((None, ts, GDN_V_W), lambda b, i: (b, i, 0)),
        ),
        out_shape=(jax.ShapeDtypeStruct((B, S, GDN_QK_W), jnp.bfloat16),
                   jax.ShapeDtypeStruct((B, S, GDN_QK_W), jnp.bfloat16),
                   jax.ShapeDtypeStruct((B, GDN_QK_W, S), jnp.bfloat16),
                   jax.ShapeDtypeStruct((B, S, GDN_V_W), jnp.bfloat16)),
        compiler_params=pltpu.CompilerParams(
            dimension_semantics=("parallel", "parallel"), vmem_limit_bytes=VMEM_LIMIT_BYTES),
    )(p, p, p, w8)


GDN_ROWS = 2 * GDN_CHUNK


def _split3(x):
    hi = x.astype(jnp.bfloat16)
    r1 = x - hi.astype(jnp.float32)
    mid = r1.astype(jnp.bfloat16)
    lo = (r1 - mid.astype(jnp.float32)).astype(jnp.bfloat16)
    return hi, mid, lo


def _gate_values(gt, alog, bias, is_decay):
    x = gt + bias
    sp = jnp.maximum(x, 0.0) + jnp.log(1.0 + jnp.exp(-jnp.abs(x)))
    return jnp.where(is_decay, -jnp.exp(alog) * sp, 1.0 / (1.0 + jnp.exp(-gt)))


def _gdn_kernel(qf_ref, kf_ref, ktf_ref, vf_ref, gf_ref, gtf_ref,
                qb_ref, kb_ref, ktb_ref, vb_ref, gb_ref, gtb_ref,
                alog_r_ref, bias_r_ref, alog_c_ref, bias_c_ref,
                of_ref, ob_ref, state_ref):
    R, C, H, D = GDN_ROWS, GDN_CHUNK, GDN_V_HEADS, GDN_HEAD_DIM
    n = pl.program_id(1)

    @pl.when(n == 0)
    def _():
        state_ref[...] = jnp.zeros_like(state_ref)

    ri = lax.broadcasted_iota(jnp.int32, (R, R), 0)
    ci = lax.broadcasted_iota(jnp.int32, (R, R), 1)
    same = (ri // C) == (ci // C)
    lower_incl = same & (ri >= ci)
    upper_incl = same & (ri <= ci)
    tri_l = lower_incl.astype(jnp.bfloat16)
    tri_u = upper_incl.astype(jnp.bfloat16)
    dec_r = (lax.broadcasted_iota(jnp.int32, (1, N_GATES), 1) // H) % 2 == 0
    dec_c = (lax.broadcasted_iota(jnp.int32, (N_GATES, 1), 0) // H) % 2 == 0
    col_chunk = lax.broadcasted_iota(jnp.int32, (1, R), 1) // C

    def cums(g_ref, gt_ref, tri, tri_t):
        gval = _gate_values(g_ref[...], alog_r_ref[...], bias_r_ref[...], dec_r)
        gval_t = _gate_values(gt_ref[...], alog_c_ref[...], bias_c_ref[...], dec_c)
        cs = sum(_dot(tri, part) for part in _split3(gval))
        cs_t = sum(_dot(part, tri_t) for part in _split3(gval_t))
        return gval, cs, cs_t

    gval_f, cs_f, cst_f = cums(gf_ref, gtf_ref, tri_l, tri_u)
    gval_b, cs_b, cst_b = cums(gb_ref, gtb_ref, tri_u, tri_l)
    dirs = (
        dict(q=qf_ref, k=kf_ref, kt=ktf_ref, v=vf_ref, gval=gval_f, cs=cs_f, cst=cst_f, goff=0,
             incl=lower_incl, strict=same & (ri > ci), order=(0, 1), last=C - 1, o=of_ref, sidx=0),
        dict(q=qb_ref, k=kb_ref, kt=ktb_ref, v=vb_ref, gval=gval_b, cs=cs_b, cst=cst_b, goff=2 * H,
             incl=upper_incl, strict=same & (ri < ci), order=(1, 0), last=0, o=ob_ref, sidx=H),
    )
    eye = (ri == ci).astype(jnp.float32)
    zeros = jnp.zeros((C, D), jnp.float32)

    probs = []
    for d in dirs:
        for hq in range(GDN_QK_HEADS):
            qc = slice(hq * D, (hq + 1) * D)
            G = _dot(jnp.concatenate([d['k'][:, qc], d['q'][:, qc]], axis=0), d['kt'][qc, :])
            for vh in range(H // GDN_QK_HEADS):
                h = hq * (H // GDN_QK_HEADS) + vh
                probs.append(dict(d=d, qc=qc, G=G, h=h, gi=d['goff'] + h, bi=d['goff'] + H + h,
                                  vcols=slice(h * D, (h + 1) * D)))
    for p in probs:
        d = p['d']
        p['gcol'] = d['cs'][:, p['gi']:p['gi'] + 1]
        p['grow'] = d['cst'][p['gi']:p['gi'] + 1, :]
        p['bcol'] = d['gval'][:, p['bi']:p['bi'] + 1]
        decay = jnp.where(d['incl'], jnp.exp(jnp.where(d['incl'], p['gcol'] - p['grow'], 0.0)), 0.0)
        L = jnp.where(d['strict'], p['bcol'] * p['G'][:R] * decay, 0.0)
        p['A'] = (p['G'][R:] * decay).astype(jnp.bfloat16)
        p['X'] = eye - L
        p['M'] = L.astype(jnp.bfloat16)
    for _ in range(C.bit_length() - 2):
        for p in probs:
            p['M'] = _dot(p['M'], p['M']).astype(jnp.bfloat16)
        for p in probs:
            p['X'] = p['X'] + _dot(p['X'].astype(jnp.bfloat16), p['M'])
    for p in probs:
        d = p['d']
        eg = jnp.exp(p['gcol'])
        vbeta = d['v'][:, p['vcols']].astype(jnp.float32) * p['bcol']
        kbg = d['k'][:, p['qc']].astype(jnp.float32) * (p['bcol'] * eg)
        UW = _dot(p['X'].astype(jnp.bfloat16), jnp.concatenate([vbeta, kbg], axis=1).astype(jnp.bfloat16))
        p['u'], p['w'] = UW[:, :D], UW[:, D:]
        p['qh'] = d['q'][:, p['qc']].astype(jnp.float32) * eg
        p['S'] = state_ref[d['sidx'] + p['h']]
    for step in range(R // C):
        for p in probs:
            d = p['d']
            c = d['order'][step]
            rows = slice(c * C, (c + 1) * C)
            r_last = c * C + d['last']
            p['glast'] = d['cs'][r_last:r_last + 1, p['gi']:p['gi'] + 1]
            p['WQ'] = _dot(jnp.concatenate([p['w'][rows], p['qh'][rows]], axis=0).astype(jnp.bfloat16),
                           p['S'].astype(jnp.bfloat16))
        for p in probs:
            d = p['d']
            c = d['order'][step]
            rows = slice(c * C, (c + 1) * C)
            vnew = p['u'][rows] - p['WQ'][:C]
            vpad = jnp.concatenate([vnew, zeros] if c == 0 else [zeros, vnew], axis=0).astype(jnp.bfloat16)
            d['o'][rows, p['vcols']] = p['WQ'][C:] + _dot(p['A'][rows], vpad)
            in_c = col_chunk == c
            scale = jnp.where(in_c, jnp.exp(jnp.where(in_c, p['glast'] - p['grow'], 0.0)), 0.0)
            ktail = (d['kt'][p['qc'], :].astype(jnp.float32) * scale).astype(jnp.bfloat16)
            p['S'] = p['S'] * jnp.exp(p['glast']) + _dot(ktail, vpad)
    for p in probs:
        state_ref[p['d']['sidx'] + p['h']] = p['S']


def _gdn(q, k, kt, v, gates, gates_t, a_log, dt_bias):
    B, S, _ = q.shape
    R, H = GDN_ROWS, GDN_V_HEADS
    nb = S // R
    z = jnp.zeros((H,), jnp.float32)
    alog = jnp.concatenate([a_log[0], z, a_log[1], z]).astype(jnp.float32)
    bias = jnp.concatenate([dt_bias[0], z, dt_bias[1], z]).astype(jnp.float32)
    fwd = lambda b, n: (b, n, 0)
    bwd = lambda b, n: (b, nb - 1 - n, 0)
    fwd_t = lambda b, n: (b, 0, n)
    bwd_t = lambda b, n: (b, 0, nb - 1 - n)

    def views(row_map, col_map):
        return [pl.BlockSpec((None, R, GDN_QK_W), row_map), pl.BlockSpec((None, R, GDN_QK_W), row_map),
                pl.BlockSpec((None, GDN_QK_W, R), col_map), pl.BlockSpec((None, R, GDN_V_W), row_map),
                pl.BlockSpec((None, R, N_GATES), row_map), pl.BlockSpec((None, N_GATES, R), col_map)]

    small = lambda shape: pl.BlockSpec(shape, lambda b, n: (0, 0))
    return pl.pallas_call(
        _gdn_kernel,
        grid=(B, nb),
        in_specs=views(fwd, fwd_t) + views(bwd, bwd_t) + [small((1, N_GATES)), small((1, N_GATES)),
                                                          small((N_GATES, 1)), small((N_GATES, 1))],
        out_specs=(pl.BlockSpec((None, R, GDN_V_W), fwd), pl.BlockSpec((None, R, GDN_V_W), bwd)),
        out_shape=(jax.ShapeDtypeStruct((B, S, GDN_V_W), jnp.float32), jax.ShapeDtypeStruct((B, S, GDN_V_W), jnp.float32)),
        scratch_shapes=[pltpu.VMEM((2 * H, GDN_HEAD_DIM, GDN_HEAD_DIM), jnp.float32)],
        compiler_params=pltpu.CompilerParams(
            dimension_semantics=("parallel", "arbitrary"), vmem_limit_bytes=VMEM_LIMIT_BYTES),
    )(q, k, kt, v, gates, gates_t, q, k, kt, v, gates, gates_t,
      alog[None, :], bias[None, :], alog[:, None], bias[:, None])


def _mix_ab(h, w_in, conv_w, a_log, dt_bias, gdn_g, w_out):
    B, S, _ = h.shape
    o1 = GDN_CONV_CH + GDN_V_W
    pad = (-AB_IN) % LANE
    w_perm = jnp.concatenate([w_in[:, :o1], w_in[:, o1 + N_GATES:], w_in[:, o1:o1 + N_GATES],
                              jnp.zeros((D_MODEL, pad), w_in.dtype)], axis=1)
    p = _mm3(h, w_perm, tn=(AB_IN + pad) // 11)
    z = p[..., GDN_CONV_CH:o1].reshape(B, S, GDN_V_HEADS, GDN_HEAD_DIM)
    u_f = p[..., o1:o1 + FNET_W]
    gates = p[..., o1 + FNET_W:o1 + FNET_W + N_GATES]
    q, k, kt, v = _gdn_prep(p, conv_w)
    o_fwd, o_bwd = _gdn(q, k, kt, v, gates, jnp.swapaxes(gates, 1, 2), a_log, dt_bias)
    o_sum = (o_fwd + o_bwd).reshape(B, S, GDN_V_HEADS, GDN_HEAD_DIM)
    o_a = (_rms(o_sum, gdn_g) * jax.nn.silu(z)).reshape(B, S, GDN_V_W)
    uf = u_f.reshape(B, S, FNET_GROUPS, FNET_GROUP_DIM)
    o_b = jnp.fft.fftn(uf, axes=(1, 3), norm='ortho').real.reshape(B, S, FNET_W)
    return _mm3(jnp.concatenate([o_a, o_b], axis=-1), w_out)


def _rope_tables(S):
    pos = jnp.arange(S, dtype=jnp.float32)
    inv = 1.0 / (ROPE_THETA ** (jnp.arange(0, ROPE_DIM, 2, dtype=jnp.float32) / ROPE_DIM))
    ang = pos[:, None] * inv[None, :]
    cos, sin = jnp.cos(ang), jnp.sin(ang)
    pad1 = jnp.ones((S, ATT_HEAD_DIM - ROPE_DIM), jnp.float32)
    pad0 = jnp.zeros((S, ATT_HEAD_DIM - ROPE_DIM), jnp.float32)
    c64 = jnp.concatenate([cos, cos, pad1], axis=1)
    s64 = jnp.concatenate([-sin, sin, pad0], axis=1)
    return jnp.concatenate([c64, c64], axis=1), jnp.concatenate([s64, s64], axis=1)


def _attn_prep_kernel(p_ref, cos_ref, sin_ref, qg_ref, kg_ref, q_ref, kt_ref, v_ref):
    lane = lax.broadcasted_iota(jnp.int32, (1, LANE), 1)
    lo = lane < ATT_HEAD_DIM
    first = (lane % ATT_HEAD_DIM) < (ROPE_DIM // 2)
    cosf, sinf = cos_ref[...], sin_ref[...]
    n_q, n_k = ATT_Q_W // LANE, ATT_KV_W // LANE
    for j in range(n_q + n_k):
        x = p_ref[:, j * LANE:(j + 1) * LANE]
        xx = x * x
        s_lo = jnp.sum(jnp.where(lo, xx, 0.0), axis=-1, keepdims=True)
        s_hi = jnp.sum(jnp.where(lo, 0.0, xx), axis=-1, keepdims=True)
        r = jnp.where(lo, lax.rsqrt(s_lo * (1.0 / ATT_HEAD_DIM) + NORM_EPS),
                      lax.rsqrt(s_hi * (1.0 / ATT_HEAD_DIM) + NORM_EPS))
        y = x * r * (qg_ref[...] if j < n_q else kg_ref[...])
        partner = jnp.where(first, pltpu.roll(y, LANE - ROPE_DIM // 2, axis=1), pltpu.roll(y, ROPE_DIM // 2, axis=1))
        y = y * cosf + partner * sinf
        if j < n_q:
            q_ref[:, j * LANE:(j + 1) * LANE] = (y * (ATT_HEAD_DIM ** -0.5)).astype(q_ref.dtype)
        else:
            jj = j - n_q
            kt_ref[jj * LANE:(jj + 1) * LANE, :] = y.T.astype(kt_ref.dtype)
    for j in range(n_k):
        x = p_ref[:, ATT_Q_W + ATT_KV_W + j * LANE:ATT_Q_W + ATT_KV_W + (j + 1) * LANE]
        sw = pltpu.roll(x, ATT_HEAD_DIM, axis=1)
        v_ref[:, (2 * j) * LANE:(2 * j + 1) * LANE] = jnp.where(lo, x, sw).astype(v_ref.dtype)
        v_ref[:, (2 * j + 1) * LANE:(2 * j + 2) * LANE] = jnp.where(lo, sw, x).astype(v_ref.dtype)


def _attn_prep(p, qn_g, kn_g, ts=256):
    B, S, W = p.shape
    cosf, sinf = _rope_tables(S)
    qg = jnp.tile(qn_g.astype(jnp.float32), 2)[None, :]
    kg = jnp.tile(kn_g.astype(jnp.float32), 2)[None, :]
    return pl.pallas_call(
        _attn_prep_kernel,
        grid=(B, S // ts),
        in_specs=[pl.BlockSpec((None, ts, W), lambda b, i: (b, i, 0)),
                  pl.BlockSpec((ts, LANE), lambda b, i: (i, 0)),
                  pl.BlockSpec((ts, LANE), lambda b, i: (i, 0)),
                  pl.BlockSpec((1, LANE), lambda b, i: (0, 0)),
                  pl.BlockSpec((1, LANE), lambda b, i: (0, 0))],
        out_specs=(pl.BlockSpec((None, ts, ATT_Q_W), lambda b, i: (b, i, 0)),
                   pl.BlockSpec((None, ATT_KV_W, ts), lambda b, i: (b, 0, i)),
                   pl.BlockSpec((None, ts, 2 * ATT_KV_W), lambda b, i: (b, i, 0))),
        out_shape=(jax.ShapeDtypeStruct((B, S, ATT_Q_W), jnp.bfloat16),
                   jax.ShapeDtypeStruct((B, ATT_KV_W, S), jnp.bfloat16),
                   jax.ShapeDtypeStruct((B, S, 2 * ATT_KV_W), jnp.bfloat16)),
        compiler_params=pltpu.CompilerParams(
            dimension_semantics=("parallel", "parallel"), vmem_limit_bytes=VMEM_LIMIT_BYTES),
    )(p, cosf, sinf, qg, kg)


def _attn_kernel(sink_ref, q_ref, ktp_ref, ktc_ref, ktn_ref, vp_ref, vc_ref, vn_ref, o_ref, *, nb):
    n = pl.program_id(1)
    W, HD = WINDOW, ATT_HEAD_DIM
    G = ATT_Q_HEADS // ATT_KV_HEADS
    half = G // 2
    rows = half * W
    qi = lax.broadcasted_iota(jnp.int32, (rows, 3 * W), 0) % W
    kt_i = lax.broadcasted_iota(jnp.int32, (rows, 3 * W), 1)
    kpos = kt_i + (n - 1) * W
    valid = (jnp.abs(kt_i - W - qi) <= W) & (kpos >= 0) & (kpos < nb * W)
    lo = lax.broadcasted_iota(jnp.int32, (1, LANE), 1) < HD
    zeros_kt = jnp.zeros((HD, 3 * W), jnp.bfloat16)

    def tiles(kh):
        return [(kh * G) // 2 + j for j in range(half)]

    def scores(kh, par):
        ksl = slice(kh * HD, (kh + 1) * HD)
        kt = jnp.concatenate([ktp_ref[ksl, :], ktc_ref[ksl, :], ktn_ref[ksl, :]], axis=1)
        ktpad = jnp.concatenate([kt, zeros_kt] if par == 0 else [zeros_kt, kt], axis=0)
        qs = jnp.concatenate([q_ref[:, t * LANE:(t + 1) * LANE] for t in tiles(kh)], axis=0)
        return _dot(qs, ktpad)

    order = [(kh, par) for kh in range(ATT_KV_HEADS) for par in range(2)]
    outs = {}
    s_next = scores(*order[0])
    for i, (kh, par) in enumerate(order):
        s = jnp.where(valid, s_next, NEG_BIG)
        if i + 1 < len(order):
            s_next = scores(*order[i + 1])
        vsl = slice(kh * LANE, (kh + 1) * LANE)
        vw = jnp.concatenate([vp_ref[:, vsl], vc_ref[:, vsl], vn_ref[:, vsl]], axis=0)
        sink = jnp.concatenate([jnp.full((W, 1), sink_ref[kh * G + 2 * j + par], jnp.float32)
                                for j in range(half)], axis=0)
        m = jnp.maximum(jnp.max(s, axis=-1, keepdims=True), sink)
        pr = jnp.exp(s - m)
        denom = jnp.sum(pr, axis=-1, keepdims=True) + jnp.exp(sink - m)
        outs[par] = _dot(pr.astype(jnp.bfloat16), vw) * (1.0 / denom)
        if par == 1:
            for j, t in enumerate(tiles(kh)):
                o_ref[:, t * LANE:(t + 1) * LANE] = jnp.where(
                    lo, outs[0][j * W:(j + 1) * W], outs[1][j * W:(j + 1) * W]).astype(o_ref.dtype)


def _attn(q, kt, v2, sinks):
    B, S, _ = q.shape
    nb = S // WINDOW
    prv = lambda n: jnp.maximum(n - 1, 0)
    nxt = lambda n: jnp.minimum(n + 1, nb - 1)
    same = lambda n: n
    kts = lambda f: pl.BlockSpec((None, ATT_KV_W, WINDOW), lambda b, n: (b, 0, f(n)))
    vs = lambda f: pl.BlockSpec((None, WINDOW, 2 * ATT_KV_W), lambda b, n: (b, f(n), 0))
    return pl.pallas_call(
        functools.partial(_attn_kernel, nb=nb),
        grid=(B, nb),
        in_specs=[pl.BlockSpec(memory_space=pltpu.SMEM),
                  pl.BlockSpec((None, WINDOW, ATT_Q_W), lambda b, n: (b, n, 0)),
                  kts(prv), kts(same), kts(nxt), vs(prv), vs(same), vs(nxt)],
        out_specs=pl.BlockSpec((None, WINDOW, ATT_Q_W), lambda b, n: (b, n, 0)),
        out_shape=jax.ShapeDtypeStruct((B, S, ATT_Q_W), jnp.bfloat16),
        compiler_params=pltpu.CompilerParams(
            dimension_semantics=("parallel", "parallel"), vmem_limit_bytes=VMEM_LIMIT_BYTES),
    )(sinks.astype(jnp.float32), q, kt, kt, kt, v2, v2, v2)


def _mix_c(h, w_in, qn_g, kn_g, sinks, w_out):
    p = _mm3(h, w_in)
    q, kt, v2 = _attn_prep(p, qn_g, kn_g)
    return _mm3(_attn(q, kt, v2, sinks), w_out)


def _ec_moe(xf, router_w, w_gate, w_up, w_down):
    T, D = xf.shape
    cap = max(1, EC_FACTOR * T // N_EXPERTS)
    aff = jax.nn.softmax((xf @ router_w).astype(jnp.float32), axis=-1)
    gate, idx = lax.top_k(aff.T, cap)
    xe = xf[idx]
    hdn = jax.nn.silu(_bmm(xe, w_gate)) * _bmm(xe, w_up)
    ye = _bmm(hdn, w_down) * gate[..., None].astype(xf.dtype)
    return jnp.zeros_like(xf).at[idx.reshape(-1)].add(ye.reshape(-1, D))


def _trunk(x, c, ada_w, ada_b, norm1_g, norm2_g, ab_w_in, ab_conv_w, gdn_a_log, gdn_dt_bias,
           gdn_norm_g, ab_w_out, c_w_in, c_q_norm_g, c_k_norm_g, c_sinks, c_w_out, router_w,
           exp_w_gate, exp_w_up, exp_w_down):
    B, S, D = x.shape
    cs = jax.nn.silu(c)
    for l in range(DEPTH):
        mod = (cs @ ada_w[l] + ada_b[l])[:, None, :]
        sh1, sc1, gt1, sh2, sc2, gt2 = jnp.split(mod, 6, axis=-1)
        hm = _rms(x, norm1_g[l]) * (1.0 + sc1) + sh1
        if l % 2 == 0:
            e = l // 2
            y = _mix_ab(hm, ab_w_in[e], ab_conv_w[e], gdn_a_log[e], gdn_dt_bias[e], gdn_norm_g[e], ab_w_out[e])
        else:
            o = l // 2
            y = _mix_c(hm, c_w_in[o], c_q_norm_g[o], c_k_norm_g[o], c_sinks[o], c_w_out[o])
        x = x + gt1 * y
        hf = _rms(x, norm2_g[l]) * (1.0 + sc2) + sh2
        x = x + gt2 * _ec_moe(hf.reshape(B * S, D), router_w[l], exp_w_gate[l], exp_w_up[l], exp_w_down[l]).reshape(B, S, D)
    return x


def kernel(x_prompt, x_sample, c_prompt, c_sample, ada_w, ada_b, norm1_g, norm2_g, ab_w_in, ab_conv_w, gdn_a_log, gdn_dt_bias, gdn_norm_g, ab_w_out, c_w_in, c_q_norm_g, c_k_norm_g, c_sinks, c_w_out, router_w, exp_w_gate, exp_w_up, exp_w_down):
    args = (ada_w, ada_b, norm1_g, norm2_g, ab_w_in, ab_conv_w, gdn_a_log, gdn_dt_bias, gdn_norm_g,
            ab_w_out, c_w_in, c_q_norm_g, c_k_norm_g, c_sinks, c_w_out, router_w, exp_w_gate,
            exp_w_up, exp_w_down)
    return (_trunk(x_prompt, c_prompt, *args), _trunk(x_sample, c_sample, *args))
```

```python
import functools

import jax
import jax.numpy as jnp
from jax import lax
from jax.experimental import pallas as pl
from jax.experimental.pallas import tpu as pltpu

D_MODEL = 2048
DEPTH = 4
GDN_QK_HEADS = 4
GDN_V_HEADS = 8
GDN_HEAD_DIM = 128
GDN_CONV = 5
GDN_CHUNK = 64
FNET_GROUPS = 4
FNET_GROUP_DIM = 256
ATT_Q_HEADS = 32
ATT_KV_HEADS = 4
ATT_HEAD_DIM = 64
WINDOW = 128
ROPE_DIM = ATT_HEAD_DIM // 4
ROPE_THETA = 500000.0
N_EXPERTS = 16
EXPERT_FF = 1024
EC_FACTOR = 2
NORM_EPS = 1e-6

GDN_QK_W = GDN_QK_HEADS * GDN_HEAD_DIM
GDN_V_W = GDN_V_HEADS * GDN_HEAD_DIM
GDN_CONV_CH = 2 * GDN_QK_W + GDN_V_W
FNET_W = FNET_GROUPS * FNET_GROUP_DIM
N_GATES = 4 * GDN_V_HEADS
AB_IN = GDN_CONV_CH + GDN_V_W + N_GATES + FNET_W
ATT_Q_W = ATT_Q_HEADS * ATT_HEAD_DIM
ATT_KV_W = ATT_KV_HEADS * ATT_HEAD_DIM

LANE = 128
SUBLANE = 8
VMEM_LIMIT_BYTES = 48 * 1024 * 1024
MOE_VMEM_LIMIT_BYTES = 56 * 1024 * 1024
NEG_BIG = -1e30


def _dot(a, b):
    return jnp.dot(a, b, preferred_element_type=jnp.float32)


def _rms_mod(x, g, sc, sh):
    y = x * lax.rsqrt(jnp.mean(x * x, axis=-1, keepdims=True) + NORM_EPS) * g
    return y * (1.0 + sc) + sh


def _norm_mm_kernel(x_ref, g_ref, sc_ref, sh_ref, w_ref, o_ref, h_ref):
    @pl.when(pl.program_id(2) == 0)
    def _():
        h_ref[...] = _rms_mod(x_ref[...], g_ref[...], sc_ref[...], sh_ref[...]).astype(h_ref.dtype)

    o_ref[...] = _dot(h_ref[...], w_ref[...])


def _norm_mm(x, g, sc, sh, w, tm=512, tn=512):
    B, S, D = x.shape
    N = w.shape[1]
    assert S % tm == 0 and N % tn == 0
    mod = lambda: pl.BlockSpec((None, 1, D), lambda b, i, j: (b, 0, 0))
    return pl.pallas_call(
        _norm_mm_kernel,
        grid=(B, S // tm, N // tn),
        in_specs=[pl.BlockSpec((None, tm, D), lambda b, i, j: (b, i, 0)),
                  pl.BlockSpec((1, D), lambda b, i, j: (0, 0)), mod(), mod(),
                  pl.BlockSpec((D, tn), lambda b, i, j: (0, j))],
        out_specs=pl.BlockSpec((None, tm, tn), lambda b, i, j: (b, i, j)),
        out_shape=jax.ShapeDtypeStruct((B, S, N), jnp.float32),
        scratch_shapes=[pltpu.VMEM((tm, D), jnp.bfloat16)],
        compiler_params=pltpu.CompilerParams(
            dimension_semantics=("parallel", "parallel", "arbitrary"), vmem_limit_bytes=VMEM_LIMIT_BYTES),
    )(x, g.astype(jnp.float32)[None, :], sc[:, None, :], sh[:, None, :], w)


def _residual_epilogue(y, x_ref, gt_ref, g2_ref, sc2_ref, sh2_ref, rw_ref, xo_ref, hf_ref, lg_ref):
    x_new = x_ref[...] + gt_ref[...] * y
    xo_ref[...] = x_new
    hf = _rms_mod(x_new, g2_ref[...], sc2_ref[...], sh2_ref[...]).astype(hf_ref.dtype)
    hf_ref[...] = hf
    lg_ref[...] = _dot(hf, rw_ref[...])


def _out_mm_kernel(a_ref, w_ref, *rest):
    _residual_epilogue(_dot(a_ref[...], w_ref[...]), *rest)


def _out_mm_ab_kernel(of_ref, ob_ref, z_ref, fft_ref, gg_ref, w_ref, *rest):
    parts = []
    for h in range(GDN_V_W // GDN_HEAD_DIM):
        cols = slice(h * GDN_HEAD_DIM, (h + 1) * GDN_HEAD_DIM)
        o = of_ref[:, cols] + ob_ref[:, cols]
        o = o * lax.rsqrt(jnp.mean(o * o, axis=-1, keepdims=True) + NORM_EPS) * gg_ref[...]
        z = z_ref[:, cols]
        parts.append((o * (z * (1.0 / (1.0 + jnp.exp(-z))))).astype(jnp.bfloat16))
    a1 = jnp.concatenate(parts, axis=1)
    y = _dot(a1, w_ref[:GDN_V_W, :]) + _dot(fft_ref[...].astype(jnp.bfloat16), w_ref[GDN_V_W:, :])
    _residual_epilogue(y, *rest)


def _out_common(B, S, D, tm):
    row = lambda w: pl.BlockSpec((None, tm, w), lambda b, i: (b, i, 0))
    mod = lambda: pl.BlockSpec((None, 1, D), lambda b, i: (b, 0, 0))
    full = lambda r, c: pl.BlockSpec((r, c), lambda b, i: (0, 0))
    tail_specs = [row(D), mod(), full(1, D), mod(), mod(), full(D, LANE)]
    out_specs = (row(D), row(D), row(LANE))
    out_shape = (jax.ShapeDtypeStruct((B, S, D), jnp.float32), jax.ShapeDtypeStruct((B, S, D), jnp.bfloat16),
                 jax.ShapeDtypeStruct((B, S, LANE), jnp.float32))
    return row, full, tail_specs, out_specs, out_shape


def _tail_args(x, gt, g2, sc2, sh2, rw):
    rwp = jnp.pad(rw.astype(jnp.bfloat16), ((0, 0), (0, LANE - rw.shape[1])))
    return (x, gt[:, None, :], g2.astype(jnp.float32)[None, :], sc2[:, None, :], sh2[:, None, :], rwp)


def _out_mm(a, w, x, gt, g2, sc2, sh2, rw, tm=256):
    B, S, D = x.shape
    row, full, tail_specs, out_specs, out_shape = _out_common(B, S, D, tm)
    return pl.pallas_call(
        _out_mm_kernel, grid=(B, S // tm),
        in_specs=[row(a.shape[-1]), full(*w.shape)] + tail_specs,
        out_specs=out_specs, out_shape=out_shape,
        compiler_params=pltpu.CompilerParams(
            dimension_semantics=("parallel", "parallel"), vmem_limit_bytes=VMEM_LIMIT_BYTES),
    )(a, w, *_tail_args(x, gt, g2, sc2, sh2, rw))


def _out_mm_ab(o_f, o_b, p, fft, gdn_g, w, x, gt, g2, sc2, sh2, rw, tm=256):
    B, S, D = x.shape
    row, full, tail_specs, out_specs, out_shape = _out_common(B, S, D, tm)
    zspec = pl.BlockSpec((None, tm, GDN_V_W), lambda b, i: (b, i, 2))
    return pl.pallas_call(
        _out_mm_ab_kernel, grid=(B, S // tm),
        in_specs=[row(GDN_V_W), row(GDN_V_W), zspec, row(FNET_W), full(1, GDN_HEAD_DIM), full(*w.shape)] + tail_specs,
        out_specs=out_specs, out_shape=out_shape,
        compiler_params=pltpu.CompilerParams(
            dimension_semantics=("parallel", "parallel"), vmem_limit_bytes=VMEM_LIMIT_BYTES),
    )(o_f, o_b, p, fft, gdn_g.astype(jnp.float32)[None, :], w, *_tail_args(x, gt, g2, sc2, sh2, rw))


def _moe_ffn_kernel(x_ref, wg_ref, wu_ref, wd_ref, gate_ref, o_ref):
    x = x_ref[...]
    h = _dot(x, wg_ref[...])
    u = _dot(x, wu_ref[...])
    hdn = (h * (1.0 / (1.0 + jnp.exp(-h))) * u).astype(jnp.bfloat16)
    o_ref[...] = _dot(hdn, wd_ref[...]) * gate_ref[...]


def _moe_ffn(xe, wg, wu, wd, gate, tm=512):
    E, cap, D = xe.shape
    F = wg.shape[2]
    return pl.pallas_call(
        _moe_ffn_kernel, grid=(E, cap // tm),
        in_specs=[pl.BlockSpec((None, tm, D), lambda e, i: (e, i, 0)),
                  pl.BlockSpec((None, D, F), lambda e, i: (e, 0, 0)),
                  pl.BlockSpec((None, D, F), lambda e, i: (e, 0, 0)),
                  pl.BlockSpec((None, F, D), lambda e, i: (e, 0, 0)),
                  pl.BlockSpec((None, tm, 1), lambda e, i: (e, i, 0))],
        out_specs=pl.BlockSpec((None, tm, D), lambda e, i: (e, i, 0)),
        out_shape=jax.ShapeDtypeStruct((E, cap, D), jnp.float32),
        compiler_params=pltpu.CompilerParams(
            dimension_semantics=("parallel", "parallel"), vmem_limit_bytes=MOE_VMEM_LIMIT_BYTES),
    )(xe, wg, wu, wd, gate[..., None])


def _gdn_prep_kernel(prev_ref, cur_ref, next_ref, w_ref, q_ref, k_ref, kt_ref, v_ref, *, ts, n_tiles):
    i = pl.program_id(1)
    has_prev = (i > 0).astype(jnp.float32)
    has_next = (i < n_tiles - 1).astype(jnp.float32)
    half = (GDN_CONV - 1) // 2
    for j in range(GDN_CONV_CH // LANE):
        cols = slice(j * LANE, (j + 1) * LANE)
        ext = jnp.concatenate([prev_ref[:, cols] * has_prev, cur_ref[:, cols], next_ref[:, cols] * has_next], axis=0)
        acc = None
        for t in range(GDN_CONV):
            shift = (half - t) % (ts + 2 * SUBLANE)
            rolled = ext if shift == 0 else pltpu.roll(ext, shift, axis=0)
            term = rolled[SUBLANE:SUBLANE + ts, :] * w_ref[t:t + 1, cols]
            acc = term if acc is None else acc + term
        y = acc * (1.0 / (1.0 + jnp.exp(-acc)))
        if j < 2 * GDN_QK_HEADS:
            n = lax.rsqrt(jnp.sum(y * y, axis=-1, keepdims=True) + NORM_EPS)
            if j < GDN_QK_HEADS:
                q_ref[:, cols] = (y * n * (GDN_HEAD_DIM ** -0.5)).astype(q_ref.dtype)
            else:
                kc = slice((j - GDN_QK_HEADS) * LANE, (j - GDN_QK_HEADS + 1) * LANE)
                kn = y * n
                k_ref[:, kc] = kn.astype(k_ref.dtype)
                kt_ref[kc, :] = kn.T.astype(kt_ref.dtype)
        else:
            vc = slice((j - 2 * GDN_QK_HEADS) * LANE, (j - 2 * GDN_QK_HEADS + 1) * LANE)
            v_ref[:, vc] = y.astype(v_ref.dtype)


def _gdn_prep(p, conv_w, ts=256):
    B, S, _ = p.shape
    n_tiles = S // ts
    r = ts // SUBLANE
    w8 = jnp.pad(conv_w.astype(jnp.float32), ((0, SUBLANE - GDN_CONV), (0, 0)))
    return pl.pallas_call(
        functools.partial(_gdn_prep_kernel, ts=ts, n_tiles=n_tiles),
        grid=(B, n_tiles),
        in_specs=[
            pl.BlockSpec((None, SUBLANE, GDN_CONV_CH), lambda b, i: (b, jnp.maximum(i * r - 1, 0), 0)),
            pl.BlockSpec((None, ts, GDN_CONV_CH), lambda b, i: (b, i, 0)),
            pl.BlockSpec((None, SUBLANE, GDN_CONV_CH), lambda b, i: (b, jnp.minimum((i + 1) * r, S // SUBLANE - 1), 0)),
            pl.BlockSpec((SUBLANE, GDN_CONV_CH), lambda b, i: (0, 0)),
        ],
        out_specs=(
            pl.BlockSpec((None, ts, GDN_QK_W), lambda b, i: (b, i, 0)),
            pl.BlockSpec((None, ts, GDN_QK_W), lambda b, i: (b, i, 0)),
            pl.BlockSpec((None, GDN_QK_W, ts), lambda b, i: (b, 0, i)),
            pl.BlockSpec((None, ts, GDN_V_W), lambda b, i: (b, i, 0)),
        ),
        out_shape=(jax.ShapeDtypeStruct((B, S, GDN_QK_W), jnp.bfloat16),
                   jax.ShapeDtypeStruct((B, S, GDN_QK_W), jnp.bfloat16),
                   jax.ShapeDtypeStruct((B, GDN_QK_W, S), jnp.bfloat16),
                   jax.ShapeDtypeStruct((B, S, GDN_V_W), jnp.bfloat16)),
        compiler_params=pltpu.CompilerParams(
            dimension_semantics=("parallel", "parallel"), vmem_limit_bytes=VMEM_LIMIT_BYTES),
    )(p, p, p, w8)


GDN_ROWS = 2 * GDN_CHUNK


def _split3(x):
    hi = x.astype(jnp.bfloat16)
    r1 = x - hi.astype(jnp.float32)
    mid = r1.astype(jnp.bfloat16)
    lo = (r1 - mid.astype(jnp.float32)).astype(jnp.bfloat16)
    return hi, mid, lo


def _gate_values(gt, alog, bias, is_decay):
    x = gt + bias
    sp = jnp.maximum(x, 0.0) + jnp.log(1.0 + jnp.exp(-jnp.abs(x)))
    return jnp.where(is_decay, -jnp.exp(alog) * sp, 1.0 / (1.0 + jnp.exp(-gt)))


def _gdn_kernel(qf_ref, kf_ref, ktf_ref, vf_ref, gf_ref, gtf_ref,
                qb_ref, kb_ref, ktb_ref, vb_ref, gb_ref, gtb_ref,
                alog_r_ref, bias_r_ref, alog_c_ref, bias_c_ref,
                of_ref, ob_ref, state_ref):
    R, C, H, D = GDN_ROWS, GDN_CHUNK, GDN_V_HEADS, GDN_HEAD_DIM
    n = pl.program_id(1)

    @pl.when(n == 0)
    def _():
        state_ref[...] = jnp.zeros_like(state_ref)

    ri = lax.broadcasted_iota(jnp.int32, (R, R), 0)
    ci = lax.broadcasted_iota(jnp.int32, (R, R), 1)
    same = (ri // C) == (ci // C)
    lower_incl = same & (ri >= ci)
    upper_incl = same & (ri <= ci)
    tri_l = lower_incl.astype(jnp.bfloat16)
    tri_u = upper_incl.astype(jnp.bfloat16)
    dec_r = (lax.broadcasted_iota(jnp.int32, (1, N_GATES), 1) // H) % 2 == 0
    dec_c = (lax.broadcasted_iota(jnp.int32, (N_GATES, 1), 0) // H) % 2 == 0
    col_chunk = lax.broadcasted_iota(jnp.int32, (1, R), 1) // C

    def cums(g_ref, gt_ref, tri, tri_t):
        gval = _gate_values(g_ref[...], alog_r_ref[...], bias_r_ref[...], dec_r)
        gval_t = _gate_values(gt_ref[...], alog_c_ref[...], bias_c_ref[...], dec_c)
        cs = sum(_dot(tri, part) for part in _split3(gval))
        cs_t = sum(_dot(part, tri_t) for part in _split3(gval_t))
        return gval, cs, cs_t

    gval_f, cs_f, cst_f = cums(gf_ref, gtf_ref, tri_l, tri_u)
    gval_b, cs_b, cst_b = cums(gb_ref, gtb_ref, tri_u, tri_l)
    dirs = (
        dict(q=qf_ref, k=kf_ref, kt=ktf_ref, v=vf_ref, gval=gval_f, cs=cs_f, cst=cst_f, goff=0,
             incl=lower_incl, strict=same & (ri > ci), order=(0, 1), last=C - 1, o=of_ref, sidx=0),
        dict(q=qb_ref, k=kb_ref, kt=ktb_ref, v=vb_ref, gval=gval_b, cs=cs_b, cst=cst_b, goff=2 * H,
             incl=upper_incl, strict=same & (ri < ci), order=(1, 0), last=0, o=ob_ref, sidx=H),
    )
    eye = (ri == ci).astype(jnp.float32)
    zeros = jnp.zeros((C, D), jnp.float32)

    probs = []
    for d in dirs:
        for hq in range(GDN_QK_HEADS):
            qc = slice(hq * D, (hq + 1) * D)
            G = _dot(jnp.concatenate([d['k'][:, qc], d['q'][:, qc]], axis=0), d['kt'][qc, :])
            for vh in range(H // GDN_QK_HEADS):
                h = hq * (H // GDN_QK_HEADS) + vh
                probs.append(dict(d=d, qc=qc, G=G, h=h, gi=d['goff'] + h, bi=d['goff'] + H + h,
                                  vcols=slice(h * D, (h + 1) * D)))
    for p in probs:
        d = p['d']
        p['gcol'] = d['cs'][:, p['gi']:p['gi'] + 1]
        p['grow'] = d['cst'][p['gi']:p['gi'] + 1, :]
        p['bcol'] = d['gval'][:, p['bi']:p['bi'] + 1]
        decay = jnp.where(d['incl'], jnp.exp(jnp.where(d['incl'], p['gcol'] - p['grow'], 0.0)), 0.0)
        L = jnp.where(d['strict'], p['bcol'] * p['G'][:R] * decay, 0.0)
        p['A'] = (p['G'][R:] * decay).astype(jnp.bfloat16)
        p['X'] = eye - L
        p['M'] = L.astype(jnp.bfloat16)
    for _ in range(C.bit_length() - 2):
        for p in probs:
            p['M'] = _dot(p['M'], p['M']).astype(jnp.bfloat16)
        for p in probs:
            p['X'] = p['X'] + _dot(p['X'].astype(jnp.bfloat16), p['M'])
    for p in probs:
        d = p['d']
        eg = jnp.exp(p['gcol'])
        vbeta = d['v'][:, p['vcols']].astype(jnp.float32) * p['bcol']
        kbg = d['k'][:, p['qc']].astype(jnp.float32) * (p['bcol'] * eg)
        UW = _dot(p['X'].astype(jnp.bfloat16), jnp.concatenate([vbeta, kbg], axis=1).astype(jnp.bfloat16))
        p['u'], p['w'] = UW[:, :D], UW[:, D:]
        p['qh'] = d['q'][:, p['qc']].astype(jnp.float32) * eg
        p['S'] = state_ref[d['sidx'] + p['h']]
    for step in range(R // C):
        for p in probs:
            d = p['d']
            c = d['order'][step]
            rows = slice(c * C, (c + 1) * C)
            r_last = c * C + d['last']
            p['glast'] = d['cs'][r_last:r_last + 1, p['gi']:p['gi'] + 1]
            p['WQ'] = _dot(jnp.concatenate([p['w'][rows], p['qh'][rows]], axis=0).astype(jnp.bfloat16),
                           p['S'].astype(jnp.bfloat16))
        for p in probs:
            d = p['d']
            c = d['order'][step]
            rows = slice(c * C, (c + 1) * C)
            vnew = p['u'][rows] - p['WQ'][:C]
            vpad = jnp.concatenate([vnew, zeros] if c == 0 else [zeros, vnew], axis=0).astype(jnp.bfloat16)
            d['o'][rows, p['vcols']] = p['WQ'][C:] + _dot(p['A'][rows], vpad)
            in_c = col_chunk == c
            scale = jnp.where(in_c, jnp.exp(jnp.where(in_c, p['glast'] - p['grow'], 0.0)), 0.0)
            ktail = (d['kt'][p['qc'], :].astype(jnp.float32) * scale).astype(jnp.bfloat16)
            p['S'] = p['S'] * jnp.exp(p['glast']) + _dot(ktail, vpad)
    for p in probs:
        state_ref[p['d']['sidx'] + p['h']] = p['S']


def _gdn(q, k, kt, v, gates, gates_t, a_log, dt_bias):
    B, S, _ = q.shape
    R, H = GDN_ROWS, GDN_V_HEADS
    nb = S // R
    z = jnp.zeros((H,), jnp.float32)
    alog = jnp.concatenate([a_log[0], z, a_log[1], z]).astype(jnp.float32)
    bias = jnp.concatenate([dt_bias[0], z, dt_bias[1], z]).astype(jnp.float32)
    fwd = lambda b, n: (b, n, 0)
    bwd = lambda b, n: (b, nb - 1 - n, 0)
    fwd_t = lambda b, n: (b, 0, n)
    bwd_t = lambda b, n: (b, 0, nb - 1 - n)

    def views(row_map, col_map):
        return [pl.BlockSpec((None, R, GDN_QK_W), row_map), pl.BlockSpec((None, R, GDN_QK_W), row_map),
                pl.BlockSpec((None, GDN_QK_W, R), col_map), pl.BlockSpec((None, R, GDN_V_W), row_map),
                pl.BlockSpec((None, R, N_GATES), row_map), pl.BlockSpec((None, N_GATES, R), col_map)]

    small = lambda shape: pl.BlockSpec(shape, lambda b, n: (0, 0))
    return pl.pallas_call(
        _gdn_kernel,
        grid=(B, nb),
        in_specs=views(fwd, fwd_t) + views(bwd, bwd_t) + [small((1, N_GATES)), small((1, N_GATES)),
                                                          small((N_GATES, 1)), small((N_GATES, 1))],
        out_specs=(pl.BlockSpec((None, R, GDN_V_W), fwd), pl.BlockSpec((None, R, GDN_V_W), bwd)),
        out_shape=(jax.ShapeDtypeStruct((B, S, GDN_V_W), jnp.float32), jax.ShapeDtypeStruct((B, S, GDN_V_W), jnp.float32)),
        scratch_shapes=[pltpu.VMEM((2 * H, GDN_HEAD_DIM, GDN_HEAD_DIM), jnp.float32)],
        compiler_params=pltpu.CompilerParams(
            dimension_semantics=("parallel", "arbitrary"), vmem_limit_bytes=VMEM_LIMIT_BYTES),
    )(q, k, kt, v, gates, gates_t, q, k, kt, v, gates, gates_t,
      alog[None, :], bias[None, :], alog[:, None], bias[:, None])


def _mix_ab(x, mods, norm_g, w_in, conv_w, a_log, dt_bias, gdn_g, w_out, tail):
    B, S, _ = x.shape
    sh1, sc1, gt1 = mods
    o1 = GDN_CONV_CH + GDN_V_W
    pad = (-AB_IN) % LANE
    w_perm = jnp.concatenate([w_in[:, :o1], w_in[:, o1 + N_GATES:], w_in[:, o1:o1 + N_GATES],
                              jnp.zeros((D_MODEL, pad), w_in.dtype)], axis=1).astype(jnp.bfloat16)
    p = _norm_mm(x, norm_g, sc1, sh1, w_perm, tn=(AB_IN + pad) // 3)
    u_f = p[..., o1:o1 + FNET_W]
    gates = p[..., o1 + FNET_W:o1 + FNET_W + N_GATES]
    q, k, kt, v = _gdn_prep(p, conv_w)
    o_fwd, o_bwd = _gdn(q, k, kt, v, gates, jnp.swapaxes(gates, 1, 2), a_log, dt_bias)
    uf = u_f.reshape(B, S, FNET_GROUPS, FNET_GROUP_DIM)
    o_b = jnp.fft.fftn(uf, axes=(1, 3), norm='ortho').real.reshape(B, S, FNET_W)
    return _out_mm_ab(o_fwd, o_bwd, p, o_b, gdn_g, w_out.astype(jnp.bfloat16), x, gt1, *tail)


def _rope_tables(S):
    pos = jnp.arange(S, dtype=jnp.float32)
    inv = 1.0 / (ROPE_THETA ** (jnp.arange(0, ROPE_DIM, 2, dtype=jnp.float32) / ROPE_DIM))
    ang = pos[:, None] * inv[None, :]
    cos, sin = jnp.cos(ang), jnp.sin(ang)
    pad1 = jnp.ones((S, ATT_HEAD_DIM - ROPE_DIM), jnp.float32)
    pad0 = jnp.zeros((S, ATT_HEAD_DIM - ROPE_DIM), jnp.float32)
    c64 = jnp.concatenate([cos, cos, pad1], axis=1)
    s64 = jnp.concatenate([-sin, sin, pad0], axis=1)
    return jnp.concatenate([c64, c64], axis=1), jnp.concatenate([s64, s64], axis=1)


def _attn_prep_kernel(p_ref, cos_ref, sin_ref, qg_ref, kg_ref, q_ref, kt_ref, v_ref):
    lane = lax.broadcasted_iota(jnp.int32, (1, LANE), 1)
    lo = lane < ATT_HEAD_DIM
    first = (lane % ATT_HEAD_DIM) < (ROPE_DIM // 2)
    cosf, sinf = cos_ref[...], sin_ref[...]
    n_q, n_k = ATT_Q_W // LANE, ATT_KV_W // LANE
    for j in range(n_q + n_k):
        x = p_ref[:, j * LANE:(j + 1) * LANE]
        xx = x * x
        s_lo = jnp.sum(jnp.where(lo, xx, 0.0), axis=-1, keepdims=True)
        s_hi = jnp.sum(jnp.where(lo, 0.0, xx), axis=-1, keepdims=True)
        r = jnp.where(lo, lax.rsqrt(s_lo * (1.0 / ATT_HEAD_DIM) + NORM_EPS),
                      lax.rsqrt(s_hi * (1.0 / ATT_HEAD_DIM) + NORM_EPS))
        y = x * r * (qg_ref[...] if j < n_q else kg_ref[...])
        partner = jnp.where(first, pltpu.roll(y, LANE - ROPE_DIM // 2, axis=1), pltpu.roll(y, ROPE_DIM // 2, axis=1))
        y = y * cosf + partner * sinf
        if j < n_q:
            q_ref[:, j * LANE:(j + 1) * LANE] = (y * (ATT_HEAD_DIM ** -0.5)).astype(q_ref.dtype)
        else:
            jj = j - n_q
            kt_ref[jj * LANE:(jj + 1) * LANE, :] = y.T.astype(kt_ref.dtype)
    for j in range(n_k):
        x = p_ref[:, ATT_Q_W + ATT_KV_W + j * LANE:ATT_Q_W + ATT_KV_W + (j + 1) * LANE]
        sw = pltpu.roll(x, ATT_HEAD_DIM, axis=1)
        v_ref[:, (2 * j) * LANE:(2 * j + 1) * LANE] = jnp.where(lo, x, sw).astype(v_ref.dtype)
        v_ref[:, (2 * j + 1) * LANE:(2 * j + 2) * LANE] = jnp.where(lo, sw, x).astype(v_ref.dtype)


def _attn_prep(p, qn_g, kn_g, ts=256):
    B, S, W = p.shape
    cosf, sinf = _rope_tables(S)
    qg = jnp.tile(qn_g.astype(jnp.float32), 2)[None, :]
    kg = jnp.tile(kn_g.astype(jnp.float32), 2)[None, :]
    return pl.pallas_call(
        _attn_prep_kernel,
        grid=(B, S // ts),
        in_specs=[pl.BlockSpec((None, ts, W), lambda b, i: (b, i, 0)),
                  pl.BlockSpec((ts, LANE), lambda b, i: (i, 0)),
                  pl.BlockSpec((ts, LANE), lambda b, i: (i, 0)),
                  pl.BlockSpec((1, LANE), lambda b, i: (0, 0)),
                  pl.BlockSpec((1, LANE), lambda b, i: (0, 0))],
        out_specs=(pl.BlockSpec((None, ts, ATT_Q_W), lambda b, i: (b, i, 0)),
                   pl.BlockSpec((None, ATT_KV_W, ts), lambda b, i: (b, 0, i)),
                   pl.BlockSpec((None, ts, 2 * ATT_KV_W), lambda b, i: (b, i, 0))),
        out_shape=(jax.ShapeDtypeStruct((B, S, ATT_Q_W), jnp.bfloat16),
                   jax.ShapeDtypeStruct((B, ATT_KV_W, S), jnp.bfloat16),
                   jax.ShapeDtypeStruct((B, S, 2 * ATT_KV_W), jnp.bfloat16)),
        compiler_params=pltpu.CompilerParams(
            dimension_semantics=("parallel", "parallel"), vmem_limit_bytes=VMEM_LIMIT_BYTES),
    )(p, cosf, sinf, qg, kg)


def _attn_kernel(sink_ref, q_ref, ktp_ref, ktc_ref, ktn_ref, vp_ref, vc_ref, vn_ref, o_ref, *, nb):
    n = pl.program_id(1)
    W, HD = WINDOW, ATT_HEAD_DIM
    G = ATT_Q_HEADS // ATT_KV_HEADS
    half = G // 2
    rows = half * W
    qi = lax.broadcasted_iota(jnp.int32, (rows, 3 * W), 0) % W
    kt_i = lax.broadcasted_iota(jnp.int32, (rows, 3 * W), 1)
    kpos = kt_i + (n - 1) * W
    valid = (jnp.abs(kt_i - W - qi) <= W) & (kpos >= 0) & (kpos < nb * W)
    lo = lax.broadcasted_iota(jnp.int32, (1, LANE), 1) < HD
    zeros_kt = jnp.zeros((HD, 3 * W), jnp.bfloat16)

    def tiles(kh):
        return [(kh * G) // 2 + j for j in range(half)]

    def scores(kh, par):
        ksl = slice(kh * HD, (kh + 1) * HD)
        kt = jnp.concatenate([ktp_ref[ksl, :], ktc_ref[ksl, :], ktn_ref[ksl, :]], axis=1)
        ktpad = jnp.concatenate([kt, zeros_kt] if par == 0 else [zeros_kt, kt], axis=0)
        qs = jnp.concatenate([q_ref[:, t * LANE:(t + 1) * LANE] for t in tiles(kh)], axis=0)
        return _dot(qs, ktpad)

    order = [(kh, par) for kh in range(ATT_KV_HEADS) for par in range(2)]
    outs = {}
    s_next = scores(*order[0])
    for i, (kh, par) in enumerate(order):
        s = jnp.where(valid, s_next, NEG_BIG)
        if i + 1 < len(order):
            s_next = scores(*order[i + 1])
        vsl = slice(kh * LANE, (kh + 1) * LANE)
        vw = jnp.concatenate([vp_ref[:, vsl], vc_ref[:, vsl], vn_ref[:, vsl]], axis=0)
        sink = jnp.concatenate([jnp.full((W, 1), sink_ref[kh * G + 2 * j + par], jnp.float32)
                                for j in range(half)], axis=0)
        m = jnp.maximum(jnp.max(s, axis=-1, keepdims=True), sink)
        pr = jnp.exp(s - m)
        denom = jnp.sum(pr, axis=-1, keepdims=True) + jnp.exp(sink - m)
        outs[par] = _dot(pr.astype(jnp.bfloat16), vw) * (1.0 / denom)
        if par == 1:
            for j, t in enumerate(tiles(kh)):
                o_ref[:, t * LANE:(t + 1) * LANE] = jnp.where(
                    lo, outs[0][j * W:(j + 1) * W], outs[1][j * W:(j + 1) * W]).astype(o_ref.dtype)


def _attn(q, kt, v2, sinks):
    B, S, _ = q.shape
    nb = S // WINDOW
    prv = lambda n: jnp.maximum(n - 1, 0)
    nxt = lambda n: jnp.minimum(n + 1, nb - 1)
    same = lambda n: n
    kts = lambda f: pl.BlockSpec((None, ATT_KV_W, WINDOW), lambda b, n: (b, 0, f(n)))
    vs = lambda f: pl.BlockSpec((None, WINDOW, 2 * ATT_KV_W), lambda b, n: (b, f(n), 0))
    return pl.pallas_call(
        functools.partial(_attn_kernel, nb=nb),
        grid=(B, nb),
        in_specs=[pl.BlockSpec(memory_space=pltpu.SMEM),
                  pl.BlockSpec((None, WINDOW, ATT_Q_W), lambda b, n: (b, n, 0)),
                  kts(prv), kts(same), kts(nxt), vs(prv), vs(same), vs(nxt)],
        out_specs=pl.BlockSpec((None, WINDOW, ATT_Q_W), lambda b, n: (b, n, 0)),
        out_shape=jax.ShapeDtypeStruct((B, S, ATT_Q_W), jnp.bfloat16),
        compiler_params=pltpu.CompilerParams(
            dimension_semantics=("parallel", "parallel"), vmem_limit_bytes=VMEM_LIMIT_BYTES),
    )(sinks.astype(jnp.float32), q, kt, kt, kt, v2, v2, v2)


def _mix_c(x, mods, norm_g, w_in, qn_g, kn_g, sinks, w_out, tail):
    sh1, sc1, gt1 = mods
    p = _norm_mm(x, norm_g, sc1, sh1, w_in.astype(jnp.bfloat16), tn=w_in.shape[1] // 2)
    q, kt, v2 = _attn_prep(p, qn_g, kn_g)
    return _out_mm(_attn(q, kt, v2, sinks), w_out.astype(jnp.bfloat16), x, gt1, *tail)


def _ec_moe(hf, logits, w_gate, w_up, w_down):
    T, D = hf.shape
    cap = max(1, EC_FACTOR * T // N_EXPERTS)
    aff = jax.nn.softmax(logits[:, :N_EXPERTS], axis=-1)
    gate, idx = lax.top_k(aff.T, cap)
    bf = lambda w: w.astype(jnp.bfloat16)
    ye = _moe_ffn(hf[idx], bf(w_gate), bf(w_up), bf(w_down), gate)
    return jnp.zeros((T, D), jnp.float32).at[idx.reshape(-1)].add(ye.reshape(-1, D))


def _trunk(x, c, ada_w, ada_b, norm1_g, norm2_g, ab_w_in, ab_conv_w, gdn_a_log, gdn_dt_bias,
           gdn_norm_g, ab_w_out, c_w_in, c_q_norm_g, c_k_norm_g, c_sinks, c_w_out, router_w,
           exp_w_gate, exp_w_up, exp_w_down):
    B, S, D = x.shape
    cs = jax.nn.silu(c)
    for l in range(DEPTH):
        mod = cs @ ada_w[l] + ada_b[l]
        sh1, sc1, gt1, sh2, sc2, gt2 = jnp.split(mod, 6, axis=-1)
        tail = (norm2_g[l], sc2, sh2, router_w[l])
        if l % 2 == 0:
            e = l // 2
            x, hf, logits = _mix_ab(x, (sh1, sc1, gt1), norm1_g[l], ab_w_in[e], ab_conv_w[e], gdn_a_log[e],
                                    gdn_dt_bias[e], gdn_norm_g[e], ab_w_out[e], tail)
        else:
            o = l // 2
            x, hf, logits = _mix_c(x, (sh1, sc1, gt1), norm1_g[l], c_w_in[o], c_q_norm_g[o], c_k_norm_g[o],
                                   c_sinks[o], c_w_out[o], tail)
        moe = _ec_moe(hf.reshape(B * S, D), logits.reshape(B * S, LANE), exp_w_gate[l], exp_w_up[l], exp_w_down[l])
        x = x + gt2[:, None, :] * moe.reshape(B, S, D)
    return x


def kernel(x_prompt, x_sample, c_prompt, c_sample, ada_w, ada_b, norm1_g, norm2_g, ab_w_in, ab_conv_w, gdn_a_log, gdn_dt_bias, gdn_norm_g, ab_w_out, c_w_in, c_q_norm_g, c_k_norm_g, c_sinks, c_w_out, router_w, exp_w_gate, exp_w_up, exp_w_down):
    args = (ada_w, ada_b, norm1_g, norm2_g, ab_w_in, ab_conv_w, gdn_a_log, gdn_dt_bias, gdn_norm_g,
            ab_w_out, c_w_in, c_q_norm_g, c_k_norm_g, c_sinks, c_w_out, router_w, exp_w_gate,
            exp_w_up, exp_w_down)
    return (_trunk(x_prompt, c_prompt, *args), _trunk(x_sample, c_sample, *args))
```

```python
import functools
import math

import jax
import jax.numpy as jnp
from jax import lax
from jax.experimental import pallas as pl
from jax.experimental.pallas import tpu as pltpu

D_MODEL = 2048
DEPTH = 4
GDN_QK_HEADS = 4
GDN_V_HEADS = 8
GDN_HEAD_DIM = 128
GDN_CONV = 5
GDN_CHUNK = 64
FNET_GROUPS = 4
FNET_GROUP_DIM = 256
ATT_Q_HEADS = 32
ATT_KV_HEADS = 4
ATT_HEAD_DIM = 64
WINDOW = 128
ROPE_DIM = ATT_HEAD_DIM // 4
ROPE_THETA = 500000.0
N_EXPERTS = 16
EXPERT_FF = 1024
EC_FACTOR = 2
NORM_EPS = 1e-6

GDN_QK_W = GDN_QK_HEADS * GDN_HEAD_DIM
GDN_V_W = GDN_V_HEADS * GDN_HEAD_DIM
GDN_CONV_CH = 2 * GDN_QK_W + GDN_V_W
FNET_W = FNET_GROUPS * FNET_GROUP_DIM
N_GATES = 4 * GDN_V_HEADS
AB_IN = GDN_CONV_CH + GDN_V_W + N_GATES + FNET_W
ATT_Q_W = ATT_Q_HEADS * ATT_HEAD_DIM
ATT_KV_W = ATT_KV_HEADS * ATT_HEAD_DIM

LANE = 128
SUBLANE = 8
VMEM_LIMIT_BYTES = 48 * 1024 * 1024
MOE_VMEM_LIMIT_BYTES = 56 * 1024 * 1024
NEG_BIG = -1e30


def _dot(a, b):
    return jnp.dot(a, b, preferred_element_type=jnp.float32)


def _rms_mod(x, g, sc, sh):
    y = x * lax.rsqrt(jnp.mean(x * x, axis=-1, keepdims=True) + NORM_EPS) * g
    return y * (1.0 + sc) + sh


def _norm_mm_kernel(x_ref, g_ref, sc_ref, sh_ref, w_ref, o_ref, h_ref):
    @pl.when(pl.program_id(2) == 0)
    def _():
        h_ref[...] = _rms_mod(x_ref[...], g_ref[...], sc_ref[...], sh_ref[...]).astype(h_ref.dtype)

    o_ref[...] = _dot(h_ref[...], w_ref[...])


def _norm_mm(x, g, sc, sh, w, tm=512, tn=512):
    B, S, D = x.shape
    N = w.shape[1]
    assert S % tm == 0 and N % tn == 0
    mod = lambda: pl.BlockSpec((None, 1, D), lambda b, i, j: (b, 0, 0))
    return pl.pallas_call(
        _norm_mm_kernel,
        grid=(B, S // tm, N // tn),
        in_specs=[pl.BlockSpec((None, tm, D), lambda b, i, j: (b, i, 0)),
                  pl.BlockSpec((1, D), lambda b, i, j: (0, 0)), mod(), mod(),
                  pl.BlockSpec((D, tn), lambda b, i, j: (0, j))],
        out_specs=pl.BlockSpec((None, tm, tn), lambda b, i, j: (b, i, j)),
        out_shape=jax.ShapeDtypeStruct((B, S, N), jnp.float32),
        scratch_shapes=[pltpu.VMEM((tm, D), jnp.bfloat16)],
        compiler_params=pltpu.CompilerParams(
            dimension_semantics=("parallel", "parallel", "arbitrary"), vmem_limit_bytes=VMEM_LIMIT_BYTES),
    )(x, g.astype(jnp.float32)[None, :], sc[:, None, :], sh[:, None, :], w)


def _residual_epilogue(y, x_ref, gt_ref, g2_ref, sc2_ref, sh2_ref, rw_ref, xo_ref, hf_ref, lg_ref):
    x_new = x_ref[...] + gt_ref[...] * y
    xo_ref[...] = x_new
    hf = _rms_mod(x_new, g2_ref[...], sc2_ref[...], sh2_ref[...]).astype(hf_ref.dtype)
    hf_ref[...] = hf
    lg_ref[...] = _dot(hf, rw_ref[...])


def _out_mm_kernel(a_ref, w_ref, *rest):
    _residual_epilogue(_dot(a_ref[...], w_ref[...]), *rest)


def _out_mm_ab_kernel(of_ref, ob_ref, z_ref, fft_ref, gg_ref, w_ref, *rest):
    parts = []
    for h in range(GDN_V_W // GDN_HEAD_DIM):
        cols = slice(h * GDN_HEAD_DIM, (h + 1) * GDN_HEAD_DIM)
        o = of_ref[:, cols] + ob_ref[:, cols]
        o = o * lax.rsqrt(jnp.mean(o * o, axis=-1, keepdims=True) + NORM_EPS) * gg_ref[...]
        z = z_ref[:, cols]
        parts.append((o * (z * (1.0 / (1.0 + jnp.exp(-z))))).astype(jnp.bfloat16))
    a1 = jnp.concatenate(parts, axis=1)
    y = _dot(a1, w_ref[:GDN_V_W, :]) + _dot(fft_ref[...].astype(jnp.bfloat16), w_ref[GDN_V_W:, :])
    _residual_epilogue(y, *rest)


def _out_common(B, S, D, tm):
    row = lambda w: pl.BlockSpec((None, tm, w), lambda b, i: (b, i, 0))
    mod = lambda: pl.BlockSpec((None, 1, D), lambda b, i: (b, 0, 0))
    full = lambda r, c: pl.BlockSpec((r, c), lambda b, i: (0, 0))
    tail_specs = [row(D), mod(), full(1, D), mod(), mod(), full(D, LANE)]
    out_specs = (row(D), row(D), row(LANE))
    out_shape = (jax.ShapeDtypeStruct((B, S, D), jnp.float32), jax.ShapeDtypeStruct((B, S, D), jnp.bfloat16),
                 jax.ShapeDtypeStruct((B, S, LANE), jnp.float32))
    return row, full, tail_specs, out_specs, out_shape


def _tail_args(x, gt, g2, sc2, sh2, rw):
    rwp = jnp.pad(rw.astype(jnp.bfloat16), ((0, 0), (0, LANE - rw.shape[1])))
    return (x, gt[:, None, :], g2.astype(jnp.float32)[None, :], sc2[:, None, :], sh2[:, None, :], rwp)


def _out_mm(a, w, x, gt, g2, sc2, sh2, rw, tm=256):
    B, S, D = x.shape
    row, full, tail_specs, out_specs, out_shape = _out_common(B, S, D, tm)
    return pl.pallas_call(
        _out_mm_kernel, grid=(B, S // tm),
        in_specs=[row(a.shape[-1]), full(*w.shape)] + tail_specs,
        out_specs=out_specs, out_shape=out_shape,
        compiler_params=pltpu.CompilerParams(
            dimension_semantics=("parallel", "parallel"), vmem_limit_bytes=VMEM_LIMIT_BYTES),
    )(a, w, *_tail_args(x, gt, g2, sc2, sh2, rw))


def _out_mm_ab(o_f, o_b, p, fft, gdn_g, w, x, gt, g2, sc2, sh2, rw, tm=256):
    B, S, D = x.shape
    row, full, tail_specs, out_specs, out_shape = _out_common(B, S, D, tm)
    zspec = pl.BlockSpec((None, tm, GDN_V_W), lambda b, i: (b, i, 2))
    return pl.pallas_call(
        _out_mm_ab_kernel, grid=(B, S // tm),
        in_specs=[row(GDN_V_W), row(GDN_V_W), zspec, row(FNET_W), full(1, GDN_HEAD_DIM), full(*w.shape)] + tail_specs,
        out_specs=out_specs, out_shape=out_shape,
        compiler_params=pltpu.CompilerParams(
            dimension_semantics=("parallel", "parallel"), vmem_limit_bytes=VMEM_LIMIT_BYTES),
    )(o_f, o_b, p, fft, gdn_g.astype(jnp.float32)[None, :], w, *_tail_args(x, gt, g2, sc2, sh2, rw))


def _moe_ffn_kernel(x_ref, wg_ref, wu_ref, wd_ref, gate_ref, o_ref):
    x = x_ref[...]
    h = _dot(x, wg_ref[...])
    u = _dot(x, wu_ref[...])
    hdn = (h * (1.0 / (1.0 + jnp.exp(-h))) * u).astype(jnp.bfloat16)
    o_ref[...] = (_dot(hdn, wd_ref[...]) * gate_ref[...]).astype(o_ref.dtype)


def _moe_ffn(xe, wg, wu, wd, gate, tm=512):
    E, cap, D = xe.shape
    F = wg.shape[2]
    return pl.pallas_call(
        _moe_ffn_kernel, grid=(E, cap // tm),
        in_specs=[pl.BlockSpec((None, tm, D), lambda e, i: (e, i, 0)),
                  pl.BlockSpec((None, D, F), lambda e, i: (e, 0, 0)),
                  pl.BlockSpec((None, D, F), lambda e, i: (e, 0, 0)),
                  pl.BlockSpec((None, F, D), lambda e, i: (e, 0, 0)),
                  pl.BlockSpec((None, tm, 1), lambda e, i: (e, i, 0))],
        out_specs=pl.BlockSpec((None, tm, D), lambda e, i: (e, i, 0)),
        out_shape=jax.ShapeDtypeStruct((E, cap, D), jnp.bfloat16),
        compiler_params=pltpu.CompilerParams(
            dimension_semantics=("parallel", "parallel"), vmem_limit_bytes=MOE_VMEM_LIMIT_BYTES),
    )(xe, wg, wu, wd, gate[..., None])


COMBINE_TOKENS = 256
COMBINE_ROWS = 512


def _combine_kernel(tile_ref, blk_ref, flag_ref, tok_ref, rows_ref, x_ref, gt_ref, o_ref, acc_ref):
    w = pl.program_id(0)
    flags = flag_ref[w]

    @pl.when((flags & 1) != 0)
    def _():
        acc_ref[...] = jnp.zeros_like(acc_ref)

    @pl.when((flags & 2) != 0)
    def _():
        tokens = tile_ref[w] * COMBINE_TOKENS + lax.broadcasted_iota(jnp.int32, (COMBINE_TOKENS, 1), 0)
        onehot = (tok_ref[...] == tokens).astype(jnp.bfloat16)
        acc_ref[...] += _dot(onehot, rows_ref[...])

    @pl.when((flags & 4) != 0)
    def _():
        o_ref[...] = x_ref[...] + gt_ref[...] * acc_ref[...]


def _moe_combine(ye, idx, x, gt):
    B, S, D = x.shape
    T = B * S
    TT, CH = COMBINE_TOKENS, COMBINE_ROWS
    n_rows = idx.size
    n_tiles = T // TT
    n_items = n_rows // CH + 2 * n_tiles
    tok = idx.reshape(-1).astype(jnp.int32)
    order = jnp.argsort(tok)
    tok_sorted = jnp.concatenate([tok[order], jnp.full((CH,), -1, jnp.int32)])
    rows_sorted = ye.reshape(n_rows, D)[jnp.concatenate([order, jnp.zeros((CH,), order.dtype)])]
    bounds = jnp.searchsorted(tok_sorted[:n_rows], jnp.arange(n_tiles + 1, dtype=jnp.int32) * TT).astype(jnp.int32)
    start, end = bounds[:-1], bounds[1:]
    b_lo = start // CH
    nblk = jnp.maximum(1, (end + CH - 1) // CH - b_lo)
    item_end = jnp.cumsum(nblk)
    item_start = item_end - nblk
    w = jnp.arange(n_items, dtype=jnp.int32)
    tile = jnp.minimum(jnp.searchsorted(item_end, w, side='right').astype(jnp.int32), n_tiles - 1)
    valid = w < item_end[-1]
    k = w - item_start[tile]
    blk = jnp.where(valid, b_lo[tile] + k, b_lo[n_tiles - 1] + nblk[n_tiles - 1] - 1)
    flags = (jnp.where(valid & (k == 0), 1, 0) | jnp.where(valid, 2, 0)
             | jnp.where(valid & (k == nblk[tile] - 1), 4, 0)).astype(jnp.int32)
    tiles_per_seq = S // TT
    grid_spec = pltpu.PrefetchScalarGridSpec(
        num_scalar_prefetch=3,
        grid=(n_items,),
        in_specs=[pl.BlockSpec((1, CH), lambda w, t, b, f: (0, b[w])),
                  pl.BlockSpec((CH, D), lambda w, t, b, f: (b[w], 0)),
                  pl.BlockSpec((TT, D), lambda w, t, b, f: (t[w], 0)),
                  pl.BlockSpec((None, 1, D), lambda w, t, b, f: (t[w] // tiles_per_seq, 0, 0))],
        out_specs=pl.BlockSpec((TT, D), lambda w, t, b, f: (t[w], 0)),
        scratch_shapes=[pltpu.VMEM((TT, D), jnp.float32)],
    )
    out = pl.pallas_call(
        _combine_kernel, grid_spec=grid_spec,
        out_shape=jax.ShapeDtypeStruct((T, D), jnp.float32),
        compiler_params=pltpu.CompilerParams(
            dimension_semantics=("arbitrary",), vmem_limit_bytes=VMEM_LIMIT_BYTES),
    )(tile, blk.astype(jnp.int32), flags, tok_sorted[None, :], rows_sorted, x.reshape(T, D), gt[:, None, :])
    return out.reshape(B, S, D)


def _cos_sin(num, den):
    ang = (2.0 * math.pi / den) * (num % den).astype(jnp.float32)
    return jnp.cos(ang), jnp.sin(ang)


def _fnet_tables(S):
    g = jnp.arange(FNET_GROUP_DIM, dtype=jnp.int32)
    cg, sg = _cos_sin(g[:, None] * g[None, :], FNET_GROUP_DIM)
    w1 = (jnp.concatenate([cg, sg], axis=1) * FNET_GROUP_DIM ** -0.5).astype(jnp.bfloat16)
    s = jnp.arange(S, dtype=jnp.int32)
    kh = jnp.arange(S // LANE, dtype=jnp.int32) * LANE
    kl = jnp.arange(LANE, dtype=jnp.int32)
    ca, sa = lax.optimization_barrier(_cos_sin(kh[:, None] * s[None, :], S))
    cb, sb = lax.optimization_barrier(_cos_sin(kl[:, None] * s[None, :], S))
    scale = S ** -0.5
    c = ((ca[:, None, :] * cb[None, :, :] - sa[:, None, :] * sb[None, :, :]) * scale).reshape(S, S)
    sn = ((sa[:, None, :] * cb[None, :, :] + ca[:, None, :] * sb[None, :, :]) * scale).reshape(S, S)
    return w1, jnp.concatenate([c, -sn], axis=1).astype(jnp.bfloat16)


def _fnet_stage1_kernel(u_ref, w_ref, o_ref):
    G = FNET_GROUP_DIM
    for g in range(FNET_GROUPS):
        r = _dot(u_ref[:, g * G:(g + 1) * G].astype(jnp.bfloat16), w_ref[...])
        o_ref[0, :, g * G:(g + 1) * G] = r[:, :G].astype(o_ref.dtype)
        o_ref[1, :, g * G:(g + 1) * G] = r[:, G:].astype(o_ref.dtype)


def _fnet_stage1(p, w1, col_block, ts=512):
    B, S, _ = p.shape
    return pl.pallas_call(
        _fnet_stage1_kernel, grid=(B, S // ts),
        in_specs=[pl.BlockSpec((None, ts, FNET_W), lambda b, i: (b, i, col_block)),
                  pl.BlockSpec((FNET_GROUP_DIM, 2 * FNET_GROUP_DIM), lambda b, i: (0, 0))],
        out_specs=pl.BlockSpec((None, 2, ts, FNET_W), lambda b, i: (b, 0, i, 0)),
        out_shape=jax.ShapeDtypeStruct((B, 2, S, FNET_W), jnp.bfloat16),
        compiler_params=pltpu.CompilerParams(
            dimension_semantics=("parallel", "parallel"), vmem_limit_bytes=VMEM_LIMIT_BYTES),
    )(p, w1)


def _fnet_stage2_kernel(a_ref, x_ref, o_ref):
    @pl.when(pl.program_id(2) == 0)
    def _():
        o_ref[...] = jnp.zeros_like(o_ref)

    o_ref[...] += _dot(a_ref[...], x_ref[...])


def _fnet_stage2(a, xs, tm=1024, tk=1024):
    B, K, N = xs.shape
    S = a.shape[0]
    tm, tk = min(tm, S), min(tk, K)
    return pl.pallas_call(
        _fnet_stage2_kernel, grid=(B, S // tm, K // tk),
        in_specs=[pl.BlockSpec((tm, tk), lambda b, i, k: (i, k)),
                  pl.BlockSpec((None, tk, N), lambda b, i, k: (b, k, 0))],
        out_specs=pl.BlockSpec((None, tm, N), lambda b, i, k: (b, i, 0)),
        out_shape=jax.ShapeDtypeStruct((B, S, N), jnp.float32),
        compiler_params=pltpu.CompilerParams(
            dimension_semantics=("parallel", "parallel", "arbitrary"), vmem_limit_bytes=VMEM_LIMIT_BYTES),
    )(a, xs)


def _fnet(p, tables, col_block):
    w1, a = tables
    B, S, _ = p.shape
    xs = _fnet_stage1(p, w1, col_block)
    return _fnet_stage2(a, xs.reshape(B, 2 * S, FNET_W))


def _gdn_prep_kernel(prev_ref, cur_ref, next_ref, w_ref, q_ref, k_ref, kt_ref, v_ref, *, ts, n_tiles):
    i = pl.program_id(1)
    has_prev = (i > 0).astype(jnp.float32)
    has_next = (i < n_tiles - 1).astype(jnp.float32)
    half = (GDN_CONV - 1) // 2
    for j in range(GDN_CONV_CH // LANE):
        cols = slice(j * LANE, (j + 1) * LANE)
        ext = jnp.concatenate([prev_ref[:, cols] * has_prev, cur_ref[:, cols], next_ref[:, cols] * has_next], axis=0)
        acc = None
        for t in range(GDN_CONV):
            shift = (half - t) % (ts + 2 * SUBLANE)
            rolled = ext if shift == 0 else pltpu.roll(ext, shift, axis=0)
            term = rolled[SUBLANE:SUBLANE + ts, :] * w_ref[t:t + 1, cols]
            acc = term if acc is None else acc + term
        y = acc * (1.0 / (1.0 + jnp.exp(-acc)))
        if j < 2 * GDN_QK_HEADS:
            n = lax.rsqrt(jnp.sum(y * y, axis=-1, keepdims=True) + NORM_EPS)
            if j < GDN_QK_HEADS:
                q_ref[:, cols] = (y * n * (GDN_HEAD_DIM ** -0.5)).astype(q_ref.dtype)
            else:
                kc = slice((j - GDN_QK_HEADS) * LANE, (j - GDN_QK_HEADS + 1) * LANE)
                kn = y * n
                k_ref[:, kc] = kn.astype(k_ref.dtype)
                kt_ref[kc, :] = kn.T.astype(kt_ref.dtype)
        else:
            vc = slice((j - 2 * GDN_QK_HEADS) * LANE, (j - 2 * GDN_QK_HEADS + 1) * LANE)
            v_ref[:, vc] = y.astype(v_ref.dtype)


def _gdn_prep(p, conv_w, ts=256):
    B, S, _ = p.shape
    n_tiles = S // ts
    r = ts // SUBLANE
    w8 = jnp.pad(conv_w.astype(jnp.float32), ((0, SUBLANE - GDN_CONV), (0, 0)))
    return pl.pallas_call(
        functools.partial(_gdn_prep_kernel, ts=ts, n_tiles=n_tiles),
        grid=(B, n_tiles),
        in_specs=[
            pl.BlockSpec((None, SUBLANE, GDN_CONV_CH), lambda b, i: (b, jnp.maximum(i * r - 1, 0), 0)),
            pl.BlockSpec((None, ts, GDN_CONV_CH), lambda b, i: (b, i, 0)),
            pl.BlockSpec((None, SUBLANE, GDN_CONV_CH), lambda b, i: (b, jnp.minimum((i + 1) * r, S // SUBLANE - 1), 0)),
            pl.BlockSpec((SUBLANE, GDN_CONV_CH), lambda b, i: (0, 0)),
        ],
        out_specs=(
            pl.BlockSpec((None, ts, GDN_QK_W), lambda b, i: (b, i, 0)),
            pl.BlockSpec((None, ts, GDN_QK_W), lambda b, i: (b, i, 0)),
            pl.BlockSpec((None, GDN_QK_W, ts), lambda b, i: (b, 0, i)),
            pl.BlockSpec((None, ts, GDN_V_W), lambda b, i: (b, i, 0)),
        ),
        out_shape=(jax.ShapeDtypeStruct((B, S, GDN_QK_W), jnp.bfloat16),
                   jax.ShapeDtypeStruct((B, S, GDN_QK_W), jnp.bfloat16),
                   jax.ShapeDtypeStruct((B, GDN_QK_W, S), jnp.bfloat16),
                   jax.ShapeDtypeStruct((B, S, GDN_V_W), jnp.bfloat16)),
        compiler_params=pltpu.CompilerParams(
            dimension_semantics=("parallel", "parallel"), vmem_limit_bytes=VMEM_LIMIT_BYTES),
    )(p, p, p, w8)


GDN_ROWS = 2 * GDN_CHUNK


def _split3(x):
    hi = x.astype(jnp.bfloat16)
    r1 = x - hi.astype(jnp.float32)
    mid = r1.astype(jnp.bfloat16)
    lo = (r1 - mid.astype(jnp.float32)).astype(jnp.bfloat16)
    return hi, mid, lo


def _gate_values(gt, alog, bias, is_decay):
    x = gt + bias
    sp = jnp.maximum(x, 0.0) + jnp.log(1.0 + jnp.exp(-jnp.abs(x)))
    return jnp.where(is_decay, -jnp.exp(alog) * sp, 1.0 / (1.0 + jnp.exp(-gt)))


def _gdn_kernel(qf_ref, kf_ref, ktf_ref, vf_ref, gf_ref, gtf_ref,
                qb_ref, kb_ref, ktb_ref, vb_ref, gb_ref, gtb_ref,
                alog_r_ref, bias_r_ref, alog_c_ref, bias_c_ref,
                of_ref, ob_ref, state_ref):
    R, C, H, D = GDN_ROWS, GDN_CHUNK, GDN_V_HEADS, GDN_HEAD_DIM
    n = pl.program_id(1)

    @pl.when(n == 0)
    def _():
        state_ref[...] = jnp.zeros_like(state_ref)

    ri = lax.broadcasted_iota(jnp.int32, (R, R), 0)
    ci = lax.broadcasted_iota(jnp.int32, (R, R), 1)
    same = (ri // C) == (ci // C)
    lower_incl = same & (ri >= ci)
    upper_incl = same & (ri <= ci)
    tri_l = lower_incl.astype(jnp.bfloat16)
    tri_u = upper_incl.astype(jnp.bfloat16)
    dec_r = (lax.broadcasted_iota(jnp.int32, (1, N_GATES), 1) // H) % 2 == 0
    dec_c = (lax.broadcasted_iota(jnp.int32, (N_GATES, 1), 0) // H) % 2 == 0
    col_chunk = lax.broadcasted_iota(jnp.int32, (1, R), 1) // C

    def cums(g_ref, gt_ref, tri, tri_t):
        gval = _gate_values(g_ref[...], alog_r_ref[...], bias_r_ref[...], dec_r)
        gval_t = _gate_values(gt_ref[...], alog_c_ref[...], bias_c_ref[...], dec_c)
        cs = sum(_dot(tri, part) for part in _split3(gval))
        cs_t = sum(_dot(part, tri_t) for part in _split3(gval_t))
        return gval, cs, cs_t

    gval_f, cs_f, cst_f = cums(gf_ref, gtf_ref, tri_l, tri_u)
    gval_b, cs_b, cst_b = cums(gb_ref, gtb_ref, tri_u, tri_l)
    dirs = (
        dict(q=qf_ref, k=kf_ref, kt=ktf_ref, v=vf_ref, gval=gval_f, cs=cs_f, cst=cst_f, goff=0,
             incl=lower_incl, strict=same & (ri > ci), order=(0, 1), last=C - 1, o=of_ref, sidx=0),
        dict(q=qb_ref, k=kb_ref, kt=ktb_ref, v=vb_ref, gval=gval_b, cs=cs_b, cst=cst_b, goff=2 * H,
             incl=upper_incl, strict=same & (ri < ci), order=(1, 0), last=0, o=ob_ref, sidx=H),
    )
    eye = (ri == ci).astype(jnp.float32)
    zeros = jnp.zeros((C, D), jnp.float32)

    probs = []
    for d in dirs:
        for hq in range(GDN_QK_HEADS):
            qc = slice(hq * D, (hq + 1) * D)
            G = _dot(jnp.concatenate([d['k'][:, qc], d['q'][:, qc]], axis=0), d['kt'][qc, :])
            for vh in range(H // GDN_QK_HEADS):
                h = hq * (H // GDN_QK_HEADS) + vh
                probs.append(dict(d=d, qc=qc, G=G, h=h, gi=d['goff'] + h, bi=d['goff'] + H + h,
                                  vcols=slice(h * D, (h + 1) * D)))
    for p in probs:
        d = p['d']
        p['gcol'] = d['cs'][:, p['gi']:p['gi'] + 1]
        p['grow'] = d['cst'][p['gi']:p['gi'] + 1, :]
        p['bcol'] = d['gval'][:, p['bi']:p['bi'] + 1]
        decay = jnp.where(d['incl'], jnp.exp(jnp.where(d['incl'], p['gcol'] - p['grow'], 0.0)), 0.0)
        L = jnp.where(d['strict'], p['bcol'] * p['G'][:R] * decay, 0.0)
        p['A'] = (p['G'][R:] * decay).astype(jnp.bfloat16)
        p['X'] = eye - L
        p['M'] = L.astype(jnp.bfloat16)
    for _ in range(C.bit_length() - 2):
        for p in probs:
            p['M'] = _dot(p['M'], p['M']).astype(jnp.bfloat16)
        for p in probs:
            p['X'] = p['X'] + _dot(p['X'].astype(jnp.bfloat16), p['M'])
    for p in probs:
        d = p['d']
        eg = jnp.exp(p['gcol'])
        vbeta = d['v'][:, p['vcols']].astype(jnp.float32) * p['bcol']
        kbg = d['k'][:, p['qc']].astype(jnp.float32) * (p['bcol'] * eg)
        UW = _dot(p['X'].astype(jnp.bfloat16), jnp.concatenate([vbeta, kbg], axis=1).astype(jnp.bfloat16))
        p['u'], p['w'] = UW[:, :D], UW[:, D:]
        p['qh'] = d['q'][:, p['qc']].astype(jnp.float32) * eg
        p['S'] = state_ref[d['sidx'] + p['h']]
    for step in range(R // C):
        for p in probs:
            d = p['d']
            c = d['order'][step]
            rows = slice(c * C, (c + 1) * C)
            r_last = c * C + d['last']
            p['glast'] = d['cs'][r_last:r_last + 1, p['gi']:p['gi'] + 1]
            p['WQ'] = _dot(jnp.concatenate([p['w'][rows], p['qh'][rows]], axis=0).astype(jnp.bfloat16),
                           p['S'].astype(jnp.bfloat16))
        for p in probs:
            d = p['d']
            c = d['order'][step]
            rows = slice(c * C, (c + 1) * C)
            vnew = p['u'][rows] - p['WQ'][:C]
            vpad = jnp.concatenate([vnew, zeros] if c == 0 else [zeros, vnew], axis=0).astype(jnp.bfloat16)
            d['o'][rows, p['vcols']] = p['WQ'][C:] + _dot(p['A'][rows], vpad)
            in_c = col_chunk == c
            scale = jnp.where(in_c, jnp.exp(jnp.where(in_c, p['glast'] - p['grow'], 0.0)), 0.0)
            ktail = (d['kt'][p['qc'], :].astype(jnp.float32) * scale).astype(jnp.bfloat16)
            p['S'] = p['S'] * jnp.exp(p['glast']) + _dot(ktail, vpad)
    for p in probs:
        state_ref[p['d']['sidx'] + p['h']] = p['S']


def _gdn(q, k, kt, v, gates, gates_t, a_log, dt_bias):
    B, S, _ = q.shape
    R, H = GDN_ROWS, GDN_V_HEADS
    nb = S // R
    z = jnp.zeros((H,), jnp.float32)
    alog = jnp.concatenate([a_log[0], z, a_log[1], z]).astype(jnp.float32)
    bias = jnp.concatenate([dt_bias[0], z, dt_bias[1], z]).astype(jnp.float32)
    fwd = lambda b, n: (b, n, 0)
    bwd = lambda b, n: (b, nb - 1 - n, 0)
    fwd_t = lambda b, n: (b, 0, n)
    bwd_t = lambda b, n: (b, 0, nb - 1 - n)

    def views(row_map, col_map):
        return [pl.BlockSpec((None, R, GDN_QK_W), row_map), pl.BlockSpec((None, R, GDN_QK_W), row_map),
                pl.BlockSpec((None, GDN_QK_W, R), col_map), pl.BlockSpec((None, R, GDN_V_W), row_map),
                pl.BlockSpec((None, R, N_GATES), row_map), pl.BlockSpec((None, N_GATES, R), col_map)]

    small = lambda shape: pl.BlockSpec(shape, lambda b, n: (0, 0))
    return pl.pallas_call(
        _gdn_kernel,
        grid=(B, nb),
        in_specs=views(fwd, fwd_t) + views(bwd, bwd_t) + [small((1, N_GATES)), small((1, N_GATES)),
                                                          small((N_GATES, 1)), small((N_GATES, 1))],
        out_specs=(pl.BlockSpec((None, R, GDN_V_W), fwd), pl.BlockSpec((None, R, GDN_V_W), bwd)),
        out_shape=(jax.ShapeDtypeStruct((B, S, GDN_V_W), jnp.float32), jax.ShapeDtypeStruct((B, S, GDN_V_W), jnp.float32)),
        scratch_shapes=[pltpu.VMEM((2 * H, GDN_HEAD_DIM, GDN_HEAD_DIM), jnp.float32)],
        compiler_params=pltpu.CompilerParams(
            dimension_semantics=("parallel", "arbitrary"), vmem_limit_bytes=VMEM_LIMIT_BYTES),
    )(q, k, kt, v, gates, gates_t, q, k, kt, v, gates, gates_t,
      alog[None, :], bias[None, :], alog[:, None], bias[:, None])


def _mix_ab(x, mods, norm_g, w_in, conv_w, a_log, dt_bias, gdn_g, w_out, tail, fnet_tables):
    B, S, _ = x.shape
    sh1, sc1, gt1 = mods
    o1 = GDN_CONV_CH + GDN_V_W
    pad = (-AB_IN) % LANE
    w_perm = jnp.concatenate([w_in[:, :o1], w_in[:, o1 + N_GATES:], w_in[:, o1:o1 + N_GATES],
                              jnp.zeros((D_MODEL, pad), w_in.dtype)], axis=1).astype(jnp.bfloat16)
    p = _norm_mm(x, norm_g, sc1, sh1, w_perm, tn=(AB_IN + pad) // 3)
    gates = p[..., o1 + FNET_W:o1 + FNET_W + N_GATES]
    q, k, kt, v = _gdn_prep(p, conv_w)
    o_fwd, o_bwd = _gdn(q, k, kt, v, gates, jnp.swapaxes(gates, 1, 2), a_log, dt_bias)
    o_b = _fnet(p, fnet_tables, o1 // FNET_W)
    return _out_mm_ab(o_fwd, o_bwd, p, o_b, gdn_g, w_out.astype(jnp.bfloat16), x, gt1, *tail)


def _rope_tables(S):
    pos = jnp.arange(S, dtype=jnp.float32)
    inv = 1.0 / (ROPE_THETA ** (jnp.arange(0, ROPE_DIM, 2, dtype=jnp.float32) / ROPE_DIM))
    ang = pos[:, None] * inv[None, :]
    cos, sin = jnp.cos(ang), jnp.sin(ang)
    pad1 = jnp.ones((S, ATT_HEAD_DIM - ROPE_DIM), jnp.float32)
    pad0 = jnp.zeros((S, ATT_HEAD_DIM - ROPE_DIM), jnp.float32)
    c64 = jnp.concatenate([cos, cos, pad1], axis=1)
    s64 = jnp.concatenate([-sin, sin, pad0], axis=1)
    return jnp.concatenate([c64, c64], axis=1), jnp.concatenate([s64, s64], axis=1)


def _attn_prep_kernel(p_ref, cos_ref, sin_ref, qg_ref, kg_ref, q_ref, kt_ref, v_ref):
    lane = lax.broadcasted_iota(jnp.int32, (1, LANE), 1)
    lo = lane < ATT_HEAD_DIM
    first = (lane % ATT_HEAD_DIM) < (ROPE_DIM // 2)
    cosf, sinf = cos_ref[...], sin_ref[...]
    n_q, n_k = ATT_Q_W // LANE, ATT_KV_W // LANE
    for j in range(n_q + n_k):
        x = p_ref[:, j * LANE:(j + 1) * LANE]
        xx = x * x
        s_lo = jnp.sum(jnp.where(lo, xx, 0.0), axis=-1, keepdims=True)
        s_hi = jnp.sum(jnp.where(lo, 0.0, xx), axis=-1, keepdims=True)
        r = jnp.where(lo, lax.rsqrt(s_lo * (1.0 / ATT_HEAD_DIM) + NORM_EPS),
                      lax.rsqrt(s_hi * (1.0 / ATT_HEAD_DIM) + NORM_EPS))
        y = x * r * (qg_ref[...] if j < n_q else kg_ref[...])
        partner = jnp.where(first, pltpu.roll(y, LANE - ROPE_DIM // 2, axis=1), pltpu.roll(y, ROPE_DIM // 2, axis=1))
        y = y * cosf + partner * sinf
        if j < n_q:
            q_ref[:, j * LANE:(j + 1) * LANE] = (y * (ATT_HEAD_DIM ** -0.5)).astype(q_ref.dtype)
        else:
            jj = j - n_q
            kt_ref[jj * LANE:(jj + 1) * LANE, :] = y.T.astype(kt_ref.dtype)
    for j in range(n_k):
        x = p_ref[:, ATT_Q_W + ATT_KV_W + j * LANE:ATT_Q_W + ATT_KV_W + (j + 1) * LANE]
        sw = pltpu.roll(x, ATT_HEAD_DIM, axis=1)
        v_ref[:, (2 * j) * LANE:(2 * j + 1) * LANE] = jnp.where(lo, x, sw).astype(v_ref.dtype)
        v_ref[:, (2 * j + 1) * LANE:(2 * j + 2) * LANE] = jnp.where(lo, sw, x).astype(v_ref.dtype)


def _attn_prep(p, qn_g, kn_g, ts=256):
    B, S, W = p.shape
    cosf, sinf = _rope_tables(S)
    qg = jnp.tile(qn_g.astype(jnp.float32), 2)[None, :]
    kg = jnp.tile(kn_g.astype(jnp.float32), 2)[None, :]
    return pl.pallas_call(
        _attn_prep_kernel,
        grid=(B, S // ts),
        in_specs=[pl.BlockSpec((None, ts, W), lambda b, i: (b, i, 0)),
                  pl.BlockSpec((ts, LANE), lambda b, i: (i, 0)),
                  pl.BlockSpec((ts, LANE), lambda b, i: (i, 0)),
                  pl.BlockSpec((1, LANE), lambda b, i: (0, 0)),
                  pl.BlockSpec((1, LANE), lambda b, i: (0, 0))],
        out_specs=(pl.BlockSpec((None, ts, ATT_Q_W), lambda b, i: (b, i, 0)),
                   pl.BlockSpec((None, ATT_KV_W, ts), lambda b, i: (b, 0, i)),
                   pl.BlockSpec((None, ts, 2 * ATT_KV_W), lambda b, i: (b, i, 0))),
        out_shape=(jax.ShapeDtypeStruct((B, S, ATT_Q_W), jnp.bfloat16),
                   jax.ShapeDtypeStruct((B, ATT_KV_W, S), jnp.bfloat16),
                   jax.ShapeDtypeStruct((B, S, 2 * ATT_KV_W), jnp.bfloat16)),
        compiler_params=pltpu.CompilerParams(
            dimension_semantics=("parallel", "parallel"), vmem_limit_bytes=VMEM_LIMIT_BYTES),
    )(p, cosf, sinf, qg, kg)


def _attn_kernel(sink_ref, q_ref, ktp_ref, ktc_ref, ktn_ref, vp_ref, vc_ref, vn_ref, o_ref, *, nb):
    n = pl.program_id(1)
    W, HD = WINDOW, ATT_HEAD_DIM
    G = ATT_Q_HEADS // ATT_KV_HEADS
    half = G // 2
    rows = half * W
    qi = lax.broadcasted_iota(jnp.int32, (rows, 3 * W), 0) % W
    kt_i = lax.broadcasted_iota(jnp.int32, (rows, 3 * W), 1)
    kpos = kt_i + (n - 1) * W
    valid = (jnp.abs(kt_i - W - qi) <= W) & (kpos >= 0) & (kpos < nb * W)
    lo = lax.broadcasted_iota(jnp.int32, (1, LANE), 1) < HD
    zeros_kt = jnp.zeros((HD, 3 * W), jnp.bfloat16)

    def tiles(kh):
        return [(kh * G) // 2 + j for j in range(half)]

    def scores(kh, par):
        ksl = slice(kh * HD, (kh + 1) * HD)
        kt = jnp.concatenate([ktp_ref[ksl, :], ktc_ref[ksl, :], ktn_ref[ksl, :]], axis=1)
        ktpad = jnp.concatenate([kt, zeros_kt] if par == 0 else [zeros_kt, kt], axis=0)
        qs = jnp.concatenate([q_ref[:, t * LANE:(t + 1) * LANE] for t in tiles(kh)], axis=0)
        return _dot(qs, ktpad)

    order = [(kh, par) for kh in range(ATT_KV_HEADS) for par in range(2)]
    outs = {}
    s_next = scores(*order[0])
    for i, (kh, par) in enumerate(order):
        s = jnp.where(valid, s_next, NEG_BIG)
        if i + 1 < len(order):
            s_next = scores(*order[i + 1])
        vsl = slice(kh * LANE, (kh + 1) * LANE)
        vw = jnp.concatenate([vp_ref[:, vsl], vc_ref[:, vsl], vn_ref[:, vsl]], axis=0)
        sink = jnp.concatenate([jnp.full((W, 1), sink_ref[kh * G + 2 * j + par], jnp.float32)
                                for j in range(half)], axis=0)
        m = jnp.maximum(jnp.max(s, axis=-1, keepdims=True), sink)
        pr = jnp.exp(s - m)
        denom = jnp.sum(pr, axis=-1, keepdims=True) + jnp.exp(sink - m)
        outs[par] = _dot(pr.astype(jnp.bfloat16), vw) * (1.0 / denom)
        if par == 1:
            for j, t in enumerate(tiles(kh)):
                o_ref[:, t * LANE:(t + 1) * LANE] = jnp.where(
                    lo, outs[0][j * W:(j + 1) * W], outs[1][j * W:(j + 1) * W]).astype(o_ref.dtype)


def _attn(q, kt, v2, sinks):
    B, S, _ = q.shape
    nb = S // WINDOW
    prv = lambda n: jnp.maximum(n - 1, 0)
    nxt = lambda n: jnp.minimum(n + 1, nb - 1)
    same = lambda n: n
    kts = lambda f: pl.BlockSpec((None, ATT_KV_W, WINDOW), lambda b, n: (b, 0, f(n)))
    vs = lambda f: pl.BlockSpec((None, WINDOW, 2 * ATT_KV_W), lambda b, n: (b, f(n), 0))
    return pl.pallas_call(
        functools.partial(_attn_kernel, nb=nb),
        grid=(B, nb),
        in_specs=[pl.BlockSpec(memory_space=pltpu.SMEM),
                  pl.BlockSpec((None, WINDOW, ATT_Q_W), lambda b, n: (b, n, 0)),
                  kts(prv), kts(same), kts(nxt), vs(prv), vs(same), vs(nxt)],
        out_specs=pl.BlockSpec((None, WINDOW, ATT_Q_W), lambda b, n: (b, n, 0)),
        out_shape=jax.ShapeDtypeStruct((B, S, ATT_Q_W), jnp.bfloat16),
        compiler_params=pltpu.CompilerParams(
            dimension_semantics=("parallel", "parallel"), vmem_limit_bytes=VMEM_LIMIT_BYTES),
    )(sinks.astype(jnp.float32), q, kt, kt, kt, v2, v2, v2)


def _mix_c(x, mods, norm_g, w_in, qn_g, kn_g, sinks, w_out, tail):
    sh1, sc1, gt1 = mods
    p = _norm_mm(x, norm_g, sc1, sh1, w_in.astype(jnp.bfloat16), tn=w_in.shape[1] // 2)
    q, kt, v2 = _attn_prep(p, qn_g, kn_g)
    return _out_mm(_attn(q, kt, v2, sinks), w_out.astype(jnp.bfloat16), x, gt1, *tail)


def _ec_moe(hf, logits, w_gate, w_up, w_down):
    T, D = hf.shape
    cap = max(1, EC_FACTOR * T // N_EXPERTS)
    aff = jax.nn.softmax(logits[:, :N_EXPERTS], axis=-1)
    gate, idx = lax.top_k(aff.T, cap)
    bf = lambda w: w.astype(jnp.bfloat16)
    return _moe_ffn(hf[idx], bf(w_gate), bf(w_up), bf(w_down), gate), idx


def _trunk(x, c, ada_w, ada_b, norm1_g, norm2_g, ab_w_in, ab_conv_w, gdn_a_log, gdn_dt_bias,
           gdn_norm_g, ab_w_out, c_w_in, c_q_norm_g, c_k_norm_g, c_sinks, c_w_out, router_w,
           exp_w_gate, exp_w_up, exp_w_down):
    B, S, D = x.shape
    cs = jax.nn.silu(c)
    fnet_tables = _fnet_tables(S)
    for l in range(DEPTH):
        mod = cs @ ada_w[l] + ada_b[l]
        sh1, sc1, gt1, sh2, sc2, gt2 = jnp.split(mod, 6, axis=-1)
        tail = (norm2_g[l], sc2, sh2, router_w[l])
        if l % 2 == 0:
            e = l // 2
            x, hf, logits = _mix_ab(x, (sh1, sc1, gt1), norm1_g[l], ab_w_in[e], ab_conv_w[e], gdn_a_log[e],
                                    gdn_dt_bias[e], gdn_norm_g[e], ab_w_out[e], tail, fnet_tables)
        else:
            o = l // 2
            x, hf, logits = _mix_c(x, (sh1, sc1, gt1), norm1_g[l], c_w_in[o], c_q_norm_g[o], c_k_norm_g[o],
                                   c_sinks[o], c_w_out[o], tail)
        ye, idx = _ec_moe(hf.reshape(B * S, D), logits.reshape(B * S, LANE), exp_w_gate[l], exp_w_up[l], exp_w_down[l])
        x = _moe_combine(ye, idx, x, gt2)
    return x


def kernel(x_prompt, x_sample, c_prompt, c_sample, ada_w, ada_b, norm1_g, norm2_g, ab_w_in, ab_conv_w, gdn_a_log, gdn_dt_bias, gdn_norm_g, ab_w_out, c_w_in, c_q_norm_g, c_k_norm_g, c_sinks, c_w_out, router_w, exp_w_gate, exp_w_up, exp_w_down):
    args = (ada_w, ada_b, norm1_g, norm2_g, ab_w_in, ab_conv_w, gdn_a_log, gdn_dt_bias, gdn_norm_g,
            ab_w_out, c_w_in, c_q_norm_g, c_k_norm_g, c_sinks, c_w_out, router_w, exp_w_gate,
            exp_w_up, exp_w_down)
    return (_trunk(x_prompt, c_prompt, *args), _trunk(x_sample, c_sample, *args))
```
